```python
import jax, jax.numpy as jnp
from jax import lax
import numpy as np

D_MODEL = 2048
BATCH = 2
SEQ = 8192
DEPTH = 2

HEAD_DIM = 64
A_HEADS = 8
A_WIDTH = A_HEADS * HEAD_DIM
DILATED_PAIRS = ((128, 1), (512, 4), (2048, 16))
B_WIDTH = 512
CONV_WIDTH = 31
C_Q_HEADS = 16
C_KV_HEADS = 2
C_WIDTH = C_Q_HEADS * HEAD_DIM
C_WINDOW = 128
MIX_WIDTH = A_WIDTH + B_WIDTH + C_WIDTH
IN_WIDTHS = (A_WIDTH, A_WIDTH, A_WIDTH,
             B_WIDTH, B_WIDTH,
             C_WIDTH, C_KV_HEADS * HEAD_DIM, C_KV_HEADS * HEAD_DIM)
IN_WIDTH = sum(IN_WIDTHS)
D_FF = -(-(8 * D_MODEL) // (3 * 256)) * 256
BLOCK = 128
EPS = 1e-6

kernel_name = "hybrid_dilated_conformer_swa_sink_block"


def _rms_norm(x, g):
    xf = x.astype(jnp.float32)
    y = xf * lax.rsqrt(jnp.mean(xf * xf, axis=-1, keepdims=True) + EPS)
    return (y * g.astype(jnp.float32)).astype(x.dtype)


def _layer_norm(x, g, b):
    xf = x.astype(jnp.float32)
    mu = jnp.mean(xf, axis=-1, keepdims=True)
    var = jnp.mean(jnp.square(xf - mu), axis=-1, keepdims=True)
    y = (xf - mu) * lax.rsqrt(var + EPS)
    return (y * g.astype(jnp.float32) + b.astype(jnp.float32)).astype(x.dtype)


def _band_attention(q, k, v, max_dist, sinks=None):
    N, L, Hq, hd = q.shape
    Hk = k.shape[2]
    G = Hq // Hk
    blk = min(BLOCK, L)
    nb = -(-L // blk)
    Lp = nb * blk
    C = blk + max_dist
    qb = jnp.pad(q, ((0, 0), (0, Lp - L), (0, 0), (0, 0))).reshape(N, nb, blk, Hk, G, hd)
    kp = jnp.pad(k, ((0, 0), (max_dist, Lp - L), (0, 0), (0, 0)))
    vp = jnp.pad(v, ((0, 0), (max_dist, Lp - L), (0, 0), (0, 0)))
    idx = jnp.arange(nb)[:, None] * blk + jnp.arange(C)[None, :]
    kb = kp[:, idx]
    vb = vp[:, idx]
    s = jnp.einsum('nbqhgd,nbkhd->nbhgqk', qb, kb,
                   preferred_element_type=jnp.float32) * (hd ** -0.5)
    qpos = jnp.arange(nb)[:, None] * blk + jnp.arange(blk)[None, :]
    kpos = idx - max_dist
    dist = qpos[:, :, None] - kpos[:, None, :]
    valid = (dist >= 0) & (dist <= max_dist) & (kpos[:, None, :] >= 0)
    s = jnp.where(valid[None, :, None, None], s, -jnp.inf)
    m = jnp.max(s, axis=-1)
    if sinks is not None:
        sk = sinks.astype(jnp.float32).reshape(Hk, G)[None, None, :, :, None]
        m = jnp.maximum(m, sk)
    p = jnp.exp(s - m[..., None])
    denom = jnp.sum(p, axis=-1)
    if sinks is not None:
        denom = denom + jnp.exp(sk - m)
    o = jnp.einsum('nbhgqk,nbkhd->nbqhgd', p, vb.astype(jnp.float32))
    denom_t = jnp.transpose(denom, (0, 1, 4, 2, 3))
    o = o / denom_t[..., None]
    lse = jnp.transpose(m, (0, 1, 4, 2, 3)) + jnp.log(denom_t)
    o = o.reshape(N, Lp, Hq, hd)[:, :L]
    lse = lse.reshape(N, Lp, Hq)[:, :L]
    return o, lse


def _dilated_mixture(q, k, v):
    B, S, H, hd = q.shape
    outs, lses = [], []
    for (w, d) in DILATED_PAIRS:
        def to_res(t):
            return t.reshape(B, S // d, d, H, hd).transpose(0, 2, 1, 3, 4).reshape(B * d, S // d, H, hd)
        o, lse = _band_attention(to_res(q), to_res(k), to_res(v), w // d)
        outs.append(o.reshape(B, d, S // d, H, hd).transpose(0, 2, 1, 3, 4).reshape(B, S, H, hd))
        lses.append(lse.reshape(B, d, S // d, H).transpose(0, 2, 1, 3).reshape(B, S, H))
    wts = jax.nn.softmax(jnp.stack(lses, axis=0), axis=0)
    return jnp.einsum('cbsh,cbshd->bshd', wts, jnp.stack(outs, axis=0))


def _conformer_conv(u, gate, conv_w, conv_b, ln_g, ln_b):
    h = u * jax.nn.sigmoid(gate)
    C = h.shape[-1]
    y = lax.conv_general_dilated(h, conv_w[:, None, :].astype(h.dtype), window_strides=(1,),
                                 padding=[(CONV_WIDTH - 1, 0)],
                                 dimension_numbers=('NWC', 'WIO', 'NWC'),
                                 feature_group_count=C)
    y = y + conv_b
    return jax.nn.silu(_layer_norm(y, ln_g, ln_b))


def setup_inputs(seed: int = 0) -> dict:
    key = jax.random.key(seed)
    ks = jax.random.split(key, 17)
    f32 = jnp.float32
    nrm = lambda k, shp: jax.random.normal(k, shp, dtype=f32)
    return {
        "x": nrm(ks[0], (BATCH, SEQ, D_MODEL)),
        "norm1_g": 1.0 + 0.02 * nrm(ks[1], (DEPTH, D_MODEL)),
        "w_in": nrm(ks[2], (DEPTH, D_MODEL, IN_WIDTH)) * D_MODEL ** -0.5,
        "a_q_g": 1.0 + 0.02 * nrm(ks[3], (DEPTH, HEAD_DIM)),
        "a_k_g": 1.0 + 0.02 * nrm(ks[4], (DEPTH, HEAD_DIM)),
        "conv_w": nrm(ks[5], (DEPTH, CONV_WIDTH, B_WIDTH)) * CONV_WIDTH ** -0.5,
        "conv_b": 0.02 * nrm(ks[6], (DEPTH, B_WIDTH)),
        "conv_ln_g": 1.0 + 0.02 * nrm(ks[7], (DEPTH, B_WIDTH)),
        "conv_ln_b": 0.02 * nrm(ks[8], (DEPTH, B_WIDTH)),
        "c_q_g": 1.0 + 0.02 * nrm(ks[9], (DEPTH, HEAD_DIM)),
        "c_k_g": 1.0 + 0.02 * nrm(ks[10], (DEPTH, HEAD_DIM)),
        "c_sinks": 0.5 * nrm(ks[11], (DEPTH, C_Q_HEADS)),
        "w_out": nrm(ks[12], (DEPTH, MIX_WIDTH, D_MODEL)) * MIX_WIDTH ** -0.5,
        "norm2_g": 1.0 + 0.02 * nrm(ks[13], (DEPTH, D_MODEL)),
        "w_gate": nrm(ks[14], (DEPTH, D_MODEL, D_FF)) * D_MODEL ** -0.5,
        "w_up": nrm(ks[15], (DEPTH, D_MODEL, D_FF)) * D_MODEL ** -0.5,
        "w_down": nrm(ks[16], (DEPTH, D_FF, D_MODEL)) * D_FF ** -0.5,
    }


def reference(x, norm1_g, w_in, a_q_g, a_k_g, conv_w, conv_b, conv_ln_g, conv_ln_b,
              c_q_g, c_k_g, c_sinks, w_out, norm2_g, w_gate, w_up, w_down):
    B, S, _ = x.shape
    splits = [int(s) for s in np.cumsum(IN_WIDTHS)[:-1]]
    for l in range(DEPTH):
        h = _rms_norm(x, norm1_g[l])
        proj = h @ w_in[l]
        aq, ak, av, bu, bg, cq, ck, cv = jnp.split(proj, splits, axis=-1)
        aq = _rms_norm(aq.reshape(B, S, A_HEADS, HEAD_DIM), a_q_g[l])
        ak = _rms_norm(ak.reshape(B, S, A_HEADS, HEAD_DIM), a_k_g[l])
        av = av.reshape(B, S, A_HEADS, HEAD_DIM)
        out_a = _dilated_mixture(aq, ak, av).reshape(B, S, A_WIDTH)
        out_b = _conformer_conv(bu, bg, conv_w[l], conv_b[l], conv_ln_g[l], conv_ln_b[l])
        cq = _rms_norm(cq.reshape(B, S, C_Q_HEADS, HEAD_DIM), c_q_g[l])
        ck = _rms_norm(ck.reshape(B, S, C_KV_HEADS, HEAD_DIM), c_k_g[l])
        cv = cv.reshape(B, S, C_KV_HEADS, HEAD_DIM)
        out_c, _ = _band_attention(cq, ck, cv, C_WINDOW - 1, sinks=c_sinks[l])
        out_c = out_c.reshape(B, S, C_WIDTH)
        mix = jnp.concatenate([out_a.astype(x.dtype), out_b.astype(x.dtype),
                               out_c.astype(x.dtype)], axis=-1)
        x = x + mix @ w_out[l]
        h2 = _rms_norm(x, norm2_g[l])
        x = x + (jax.nn.silu(h2 @ w_gate[l]) * (h2 @ w_up[l])) @ w_down[l]
    return x
```

```python
import functools

import jax
import jax.numpy as jnp
from jax import lax
from jax.experimental import pallas as pl
from jax.experimental.pallas import tpu as pltpu

HEAD_DIM = 64
A_HEADS = 8
A_WIDTH = A_HEADS * HEAD_DIM
DILATED_PAIRS = ((128, 1), (512, 4), (2048, 16))
B_WIDTH = 512
CONV_WIDTH = 31
C_Q_HEADS = 16
C_KV_HEADS = 2
C_WIDTH = C_Q_HEADS * HEAD_DIM
C_KV_WIDTH = C_KV_HEADS * HEAD_DIM
C_WINDOW = 128
IN_WIDTH = 3 * A_WIDTH + 2 * B_WIDTH + C_WIDTH + 2 * C_KV_WIDTH
EPS = 1e-6

A_Q_OFF, A_K_OFF, A_V_OFF = 0, A_WIDTH, 2 * A_WIDTH
B_U_OFF, B_G_OFF = 3 * A_WIDTH, 3 * A_WIDTH + B_WIDTH
C_Q_OFF = 3 * A_WIDTH + 2 * B_WIDTH
C_K_OFF = C_Q_OFF + C_WIDTH
C_V_OFF = C_K_OFF + C_KV_WIDTH

ATT_BLOCK = 128
HEAD_CHUNK = 256
HEADS_PER_CHUNK = HEAD_CHUNK // HEAD_DIM
CONV_HALO = 32
MASKED_SCORE = -1e30
V7X_VMEM_LIMIT = 56 * 1024 * 1024

_bf16 = jnp.bfloat16
_f32 = jnp.float32


def _sigmoid(v):
    return 1.0 / (1.0 + jnp.exp(-v))


def _in_proj_kernel(x_ref, g_ref, w_ref, e_ref, gain_ref, flag_ref, o_ref, h_ref, *,
                    chunks_per_tile, normed_lo_end, normed_hi_start):
    j = pl.program_id(1)

    @pl.when(j == 0)
    def _():
        x = x_ref[...]
        ms = jnp.mean(x * x, axis=-1, keepdims=True)
        h_ref[...] = (x * lax.rsqrt(ms + EPS) * g_ref[...]).astype(_bf16)

    for c in range(chunks_per_tile):
        sl = slice(c * HEAD_CHUNK, (c + 1) * HEAD_CHUNK)
        a = jnp.dot(h_ref[...], w_ref[:, sl], preferred_element_type=_f32)
        chunk = j * chunks_per_tile + c
        normed = jnp.logical_or(chunk < normed_lo_end, chunk >= normed_hi_start)

        @pl.when(normed)
        def _():
            ss = jnp.dot((a * a).astype(_bf16), e_ref[...], preferred_element_type=_f32)
            yn = a * lax.rsqrt(ss * (1.0 / HEAD_DIM) + EPS) * gain_ref[:, sl]
            o_ref[:, sl] = jnp.where(flag_ref[:, sl] > 0.0, yn, a).astype(o_ref.dtype)

        @pl.when(jnp.logical_not(normed))
        def _():
            o_ref[:, sl] = a.astype(o_ref.dtype)


def _in_proj(x, g, w, head_ones, gain, flag, *, tm, tn):
    m, d = x.shape
    n = w.shape[1]
    assert m % tm == 0 and n % tn == 0 and tn % HEAD_CHUNK == 0
    kern = functools.partial(
        _in_proj_kernel, chunks_per_tile=tn // HEAD_CHUNK,
        normed_lo_end=A_V_OFF // HEAD_CHUNK, normed_hi_start=C_Q_OFF // HEAD_CHUNK)
    return pl.pallas_call(
        kern,
        out_shape=jax.ShapeDtypeStruct((m, n), _bf16),
        grid=(m // tm, n // tn),
        in_specs=[
            pl.BlockSpec((tm, d), lambda i, j: (i, 0)),
            pl.BlockSpec((1, d), lambda i, j: (0, 0)),
            pl.BlockSpec((d, tn), lambda i, j: (0, j)),
            pl.BlockSpec((HEAD_CHUNK, HEAD_CHUNK), lambda i, j: (0, 0)),
            pl.BlockSpec((1, tn), lambda i, j: (0, j)),
            pl.BlockSpec((1, tn), lambda i, j: (0, j)),
        ],
        out_specs=pl.BlockSpec((tm, tn), lambda i, j: (i, j)),
        scratch_shapes=[pltpu.VMEM((tm, d), _bf16)],
        compiler_params=pltpu.CompilerParams(
            dimension_semantics=("parallel", "arbitrary"), vmem_limit_bytes=V7X_VMEM_LIMIT),
        name="in_proj",
    )(x, g, w, head_ones, gain, flag)


def _band_mask(i, max_dist):
    t = ATT_BLOCK
    row = lax.broadcasted_iota(jnp.int32, (t, 2 * t), 0)
    col = lax.broadcasted_iota(jnp.int32, (t, 2 * t), 1)
    dist = row + t - col
    valid = jnp.logical_and(dist >= 0, dist <= max_dist)
    return jnp.logical_and(valid, jnp.logical_or(col >= t, i > 0))


def _attn_chunk(q, kcat, vcat, valid, sinks):
    lane_head = lax.broadcasted_iota(jnp.int32, (1, HEAD_CHUNK), 1) // HEAD_DIM
    out = jnp.zeros((q.shape[0], HEAD_CHUNK), _f32)
    lse = jnp.zeros((q.shape[0], HEAD_CHUNK), _f32)
    for h in range(HEADS_PER_CHUNK):
        sel = lane_head == h
        qm = jnp.where(sel, q, jnp.zeros_like(q))
        s = lax.dot_general(qm, kcat, (((1,), (1,)), ((), ())), preferred_element_type=_f32)
        s = jnp.where(valid, s, MASKED_SCORE)
        m = jnp.max(s, axis=-1, keepdims=True)
        if sinks is not None:
            m = jnp.maximum(m, sinks[h])
        p = jnp.exp(s - m)
        l = jnp.sum(p, axis=-1, keepdims=True)
        if sinks is not None:
            l = l + jnp.exp(sinks[h] - m)
        o = jnp.dot(p.astype(_bf16), vcat, preferred_element_type=_f32)
        out = jnp.where(sel, o * (1.0 / l), out)
        lse = jnp.where(sel, m + jnp.log(l), lse)
    return out, lse


def _attn_a_kernel(q_ref, kp_ref, kc_ref, vp_ref, vc_ref, o_ref, lse_ref, *, max_dist):
    i = pl.program_id(3)
    kcat = jnp.concatenate([kp_ref[0], kc_ref[0]], axis=0)
    vcat = jnp.concatenate([vp_ref[0], vc_ref[0]], axis=0)
    out, lse = _attn_chunk(q_ref[0], kcat, vcat, _band_mask(i, max_dist), None)
    o_ref[0] = out.astype(o_ref.dtype)
    lse_ref[0] = lse


def _attn_a(proj, window, dilation):
    b, s, _ = proj.shape
    d = dilation
    max_dist = window // d
    assert max_dist <= ATT_BLOCK and s % (d * ATT_BLOCK) == 0
    sd = s // d
    view = proj.reshape(b, sd, d * IN_WIDTH)
    cpr = IN_WIDTH // HEAD_CHUNK
    qc, kc, vc = A_Q_OFF // HEAD_CHUNK, A_K_OFF // HEAD_CHUNK, A_V_OFF // HEAD_CHUNK
    n_chunks = A_WIDTH // HEAD_CHUNK
    blk = (1, ATT_BLOCK, HEAD_CHUNK)

    def cur(off):
        return pl.BlockSpec(blk, lambda bb, r, c, i: (bb, i, r * cpr + off + c))

    def prev(off):
        return pl.BlockSpec(blk, lambda bb, r, c, i: (bb, jnp.maximum(i - 1, 0), r * cpr + off + c))

    out_spec = pl.BlockSpec(blk, lambda bb, r, c, i: (bb, i, r * n_chunks + c))
    o, lse = pl.pallas_call(
        functools.partial(_attn_a_kernel, max_dist=max_dist),
        out_shape=(jax.ShapeDtypeStruct((b, sd, d * A_WIDTH), _bf16),
                   jax.ShapeDtypeStruct((b, sd, d * A_WIDTH), _f32)),
        grid=(b, d, n_chunks, sd // ATT_BLOCK),
        in_specs=[cur(qc), prev(kc), cur(kc), prev(vc), cur(vc)],
        out_specs=(out_spec, out_spec),
        compiler_params=pltpu.CompilerParams(
            dimension_semantics=("parallel", "parallel", "parallel", "arbitrary"),
            vmem_limit_bytes=V7X_VMEM_LIMIT),
        name=f"attn_a_d{d}",
    )(view, view, view, view, view)
    return o.reshape(b * s, A_WIDTH), lse.reshape(b * s, A_WIDTH)


def _attn_c_kernel(sink_ref, q_ref, kp_ref, kc_ref, vp_ref, vc_ref, o_ref, *, max_dist):
    g = pl.program_id(1)
    i = pl.program_id(2)
    src = lax.broadcasted_iota(jnp.int32, (C_KV_WIDTH, HEAD_CHUNK), 0)
    dst = lax.broadcasted_iota(jnp.int32, (C_KV_WIDTH, HEAD_CHUNK), 1)
    rep = (src == g * HEAD_DIM + dst % HEAD_DIM).astype(_bf16)
    k2 = jnp.concatenate([kp_ref[0], kc_ref[0]], axis=0)
    v2 = jnp.concatenate([vp_ref[0], vc_ref[0]], axis=0)
    kcat = jnp.dot(k2, rep, preferred_element_type=_f32).astype(_bf16)
    vcat = jnp.dot(v2, rep, preferred_element_type=_f32).astype(_bf16)
    valid = _band_mask(i, max_dist)
    heads_per_group = C_Q_HEADS // C_KV_HEADS
    for c in range(heads_per_group // HEADS_PER_CHUNK):
        sl = slice(c * HEAD_CHUNK, (c + 1) * HEAD_CHUNK)
        sinks = [sink_ref[g * heads_per_group + c * HEADS_PER_CHUNK + h] for h in range(HEADS_PER_CHUNK)]
        out, _ = _attn_chunk(q_ref[0, :, sl], kcat, vcat, valid, sinks)
        o_ref[0, :, sl] = out.astype(o_ref.dtype)


def _attn_c(proj, sinks):
    b, s, _ = proj.shape
    group_width = C_WIDTH // C_KV_HEADS
    assert s % ATT_BLOCK == 0 and C_Q_OFF % group_width == 0
    assert C_K_OFF % C_KV_WIDTH == 0 and C_V_OFF % C_KV_WIDTH == 0
    q_blk, k_blk, v_blk = C_Q_OFF // group_width, C_K_OFF // C_KV_WIDTH, C_V_OFF // C_KV_WIDTH
    kv = (1, ATT_BLOCK, C_KV_WIDTH)

    def cur(col):
        return pl.BlockSpec(kv, lambda bb, g, i: (bb, i, col))

    def prev(col):
        return pl.BlockSpec(kv, lambda bb, g, i: (bb, jnp.maximum(i - 1, 0), col))

    return pl.pallas_call(
        functools.partial(_attn_c_kernel, max_dist=C_WINDOW - 1),
        out_shape=jax.ShapeDtypeStruct((b, s, C_WIDTH), _bf16),
        grid=(b, C_KV_HEADS, s // ATT_BLOCK),
        in_specs=[
            pl.BlockSpec(memory_space=pltpu.SMEM),
            pl.BlockSpec((1, ATT_BLOCK, group_width), lambda bb, g, i: (bb, i, q_blk + g)),
            prev(k_blk), cur(k_blk), prev(v_blk), cur(v_blk),
        ],
        out_specs=pl.BlockSpec((1, ATT_BLOCK, group_width), lambda bb, g, i: (bb, i, g)),
        compiler_params=pltpu.CompilerParams(
            dimension_semantics=("parallel", "parallel", "arbitrary"),
            vmem_limit_bytes=V7X_VMEM_LIMIT),
        name="attn_c",
    )(sinks, proj, proj, proj, proj, proj).reshape(b * s, C_WIDTH)


CONV_ROWS = 64
LANES = 128


def _conv_kernel(up_ref, gp_ref, uc_ref, gc_ref, w_ref, b_ref, lg_ref, lb_ref, o_ref, hext_ref, y_ref, *, tc):
    i = pl.program_id(1)
    hp = up_ref[0].astype(_f32) * _sigmoid(gp_ref[0].astype(_f32))
    hext_ref[0:CONV_HALO, :] = jnp.where(i > 0, hp, 0.0)
    hext_ref[CONV_HALO:CONV_HALO + tc, :] = uc_ref[0].astype(_f32) * _sigmoid(gc_ref[0].astype(_f32))
    first = CONV_HALO - (CONV_WIDTH - 1)
    for cg in range(B_WIDTH // LANES):
        cs = slice(cg * LANES, (cg + 1) * LANES)
        for rg in range(tc // CONV_ROWS):
            r0 = rg * CONV_ROWS
            acc = jnp.zeros((CONV_ROWS, LANES), _f32)
            for k in range(CONV_WIDTH):
                acc = acc + w_ref[k:k + 1, cs] * hext_ref[first + r0 + k:first + r0 + k + CONV_ROWS, cs]
            y_ref[r0:r0 + CONV_ROWS, cs] = acc + b_ref[:, cs]
    y = y_ref[...]
    mu = jnp.mean(y, axis=-1, keepdims=True)
    yc = y - mu
    var = jnp.mean(yc * yc, axis=-1, keepdims=True)
    z = yc * lax.rsqrt(var + EPS) * lg_ref[...] + lb_ref[...]
    o_ref[0] = (z * _sigmoid(z)).astype(o_ref.dtype)


def _conv(proj, conv_w, conv_b, ln_g, ln_b, *, tc):
    b, s, _ = proj.shape
    assert s % tc == 0 and tc % CONV_HALO == 0 and CONV_HALO >= CONV_WIDTH - 1
    assert B_U_OFF % B_WIDTH == 0 and B_G_OFF % B_WIDTH == 0
    u_blk, g_blk = B_U_OFF // B_WIDTH, B_G_OFF // B_WIDTH
    halo_per_block = tc // CONV_HALO

    def cur(col):
        return pl.BlockSpec((1, tc, B_WIDTH), lambda bb, i: (bb, i, col))

    def halo(col):
        return pl.BlockSpec((1, CONV_HALO, B_WIDTH),
                            lambda bb, i: (bb, jnp.maximum(i * halo_per_block - 1, 0), col))

    def vec():
        return pl.BlockSpec((1, B_WIDTH), lambda bb, i: (0, 0))

    return pl.pallas_call(
        functools.partial(_conv_kernel, tc=tc),
        out_shape=jax.ShapeDtypeStruct((b, s, B_WIDTH), _bf16),
        grid=(b, s // tc),
        in_specs=[halo(u_blk), halo(g_blk), cur(u_blk), cur(g_blk),
                  pl.BlockSpec((CONV_WIDTH, B_WIDTH), lambda bb, i: (0, 0)), vec(), vec(), vec()],
        out_specs=pl.BlockSpec((1, tc, B_WIDTH), lambda bb, i: (bb, i, 0)),
        scratch_shapes=[pltpu.VMEM((CONV_HALO + tc, B_WIDTH), _f32), pltpu.VMEM((tc, B_WIDTH), _f32)],
        compiler_params=pltpu.CompilerParams(
            dimension_semantics=("parallel", "arbitrary"), vmem_limit_bytes=V7X_VMEM_LIMIT),
        name="conformer_conv",
    )(proj, proj, proj, proj, conv_w, conv_b, ln_g, ln_b).reshape(b * s, B_WIDTH)


def _out_proj_kernel(o1_ref, o2_ref, o3_ref, l1_ref, l2_ref, l3_ref, ob_ref, oc_ref, x_ref, w_ref, xo_ref):
    l1, l2, l3 = l1_ref[...], l2_ref[...], l3_ref[...]
    m = jnp.maximum(jnp.maximum(l1, l2), l3)
    e1, e2, e3 = jnp.exp(l1 - m), jnp.exp(l2 - m), jnp.exp(l3 - m)
    mixed = e1 * o1_ref[...].astype(_f32) + e2 * o2_ref[...].astype(_f32) + e3 * o3_ref[...].astype(_f32)
    oa = (mixed * (1.0 / (e1 + e2 + e3))).astype(_bf16)
    acc = jnp.dot(oa, w_ref[0:A_WIDTH, :], preferred_element_type=_f32)
    acc = acc + jnp.dot(ob_ref[...], w_ref[A_WIDTH:A_WIDTH + B_WIDTH, :], preferred_element_type=_f32)
    acc = acc + jnp.dot(oc_ref[...], w_ref[A_WIDTH + B_WIDTH:, :], preferred_element_type=_f32)
    xo_ref[...] = x_ref[...] + acc


def _out_proj(oa, la, ob, oc, x, w, *, tm):
    m, d = x.shape
    assert m % tm == 0

    def rows(width):
        return pl.BlockSpec((tm, width), lambda i: (i, 0))

    return pl.pallas_call(
        _out_proj_kernel,
        out_shape=jax.ShapeDtypeStruct((m, d), _f32),
        grid=(m // tm,),
        in_specs=[rows(A_WIDTH)] * 6 + [rows(B_WIDTH), rows(C_WIDTH), rows(d),
                                        pl.BlockSpec(w.shape, lambda i: (0, 0))],
        out_specs=rows(d),
        compiler_params=pltpu.CompilerParams(
            dimension_semantics=("parallel",), vmem_limit_bytes=V7X_VMEM_LIMIT),
        name="out_proj",
    )(*oa, *la, ob, oc, x, w)


def _ffn_kernel(x_ref, g_ref, wg_ref, wu_ref, wd_ref, o_ref, h_ref):
    f = pl.program_id(1)

    @pl.when(f == 0)
    def _():
        x = x_ref[...]
        ms = jnp.mean(x * x, axis=-1, keepdims=True)
        h_ref[...] = (x * lax.rsqrt(ms + EPS) * g_ref[...]).astype(_bf16)
        o_ref[...] = x

    h = h_ref[...]
    gate = jnp.dot(h, wg_ref[...], preferred_element_type=_f32)
    up = jnp.dot(h, wu_ref[...], preferred_element_type=_f32)
    act = (gate * _sigmoid(gate) * up).astype(_bf16)
    o_ref[...] += jnp.dot(act, wd_ref[...], preferred_element_type=_f32)


def _ffn(x, g, wg, wu, wd, *, tm, tf):
    m, d = x.shape
    ff = wg.shape[1]
    assert m % tm == 0 and ff % tf == 0
    return pl.pallas_call(
        _ffn_kernel,
        out_shape=jax.ShapeDtypeStruct((m, d), _f32),
        grid=(m // tm, ff // tf),
        in_specs=[
            pl.BlockSpec((tm, d), lambda i, f: (i, 0)),
            pl.BlockSpec((1, d), lambda i, f: (0, 0)),
            pl.BlockSpec((d, tf), lambda i, f: (0, f)),
            pl.BlockSpec((d, tf), lambda i, f: (0, f)),
            pl.BlockSpec((tf, d), lambda i, f: (f, 0)),
        ],
        out_specs=pl.BlockSpec((tm, d), lambda i, f: (i, 0)),
        scratch_shapes=[pltpu.VMEM((tm, d), _bf16)],
        compiler_params=pltpu.CompilerParams(
            dimension_semantics=("parallel", "arbitrary"), vmem_limit_bytes=V7X_VMEM_LIMIT),
        name="swiglu_ffn",
    )(x, g, wg, wu, wd)


def _tile(m, pref):
    t = pref
    while m % t:
        t //= 2
    return t


def _qk_gain_and_flag(a_q_g, a_k_g, c_q_g, c_k_g):
    scale = HEAD_DIM ** -0.5
    ones = lambda n: jnp.ones((n,), _f32)
    zeros = lambda n: jnp.zeros((n,), _f32)
    gain = jnp.concatenate([
        jnp.tile(a_q_g * scale, A_HEADS), jnp.tile(a_k_g, A_HEADS), ones(A_WIDTH + 2 * B_WIDTH),
        jnp.tile(c_q_g * scale, C_Q_HEADS), jnp.tile(c_k_g, C_KV_HEADS), ones(C_KV_WIDTH)])
    flag = jnp.concatenate([ones(2 * A_WIDTH), zeros(A_WIDTH + 2 * B_WIDTH),
                            ones(C_WIDTH + C_KV_WIDTH), zeros(C_KV_WIDTH)])
    return gain.reshape(1, IN_WIDTH), flag.reshape(1, IN_WIDTH)


def kernel(x, norm1_g, w_in, a_q_g, a_k_g, conv_w, conv_b, conv_ln_g, conv_ln_b,
           c_q_g, c_k_g, c_sinks, w_out, norm2_g, w_gate, w_up, w_down):
    b, s, d = x.shape
    m = b * s
    depth = w_in.shape[0]
    lane = jnp.arange(HEAD_CHUNK) // HEAD_DIM
    head_ones = (lane[:, None] == lane[None, :]).astype(_bf16)
    xf = x.reshape(m, d)
    for l in range(depth):
        gain, flag = _qk_gain_and_flag(a_q_g[l], a_k_g[l], c_q_g[l], c_k_g[l])
        proj = _in_proj(xf, norm1_g[l].reshape(1, d), w_in[l].astype(_bf16), head_ones, gain, flag,
                        tm=_tile(m, 1024), tn=768)
        proj = proj.reshape(b, s, IN_WIDTH)
        oa, la = zip(*[_attn_a(proj, w, dil) for (w, dil) in DILATED_PAIRS])
        ob = _conv(proj, conv_w[l], conv_b[l].reshape(1, -1), conv_ln_g[l].reshape(1, -1),
                   conv_ln_b[l].reshape(1, -1), tc=256)
        oc = _attn_c(proj, c_sinks[l])
        xf = _out_proj(oa, la, ob, oc, xf, w_out[l].astype(_bf16), tm=_tile(m, 256))
        xf = _ffn(xf, norm2_g[l].reshape(1, d), w_gate[l].astype(_bf16), w_up[l].astype(_bf16),
                  w_down[l].astype(_bf16), tm=_tile(m, 512), tf=512)
    return xf.reshape(b, s, d)
```

```python
import functools

import jax
import jax.numpy as jnp
from jax import lax
from jax.experimental import pallas as pl
from jax.experimental.pallas import tpu as pltpu

HEAD_DIM = 64
A_HEADS = 8
A_WIDTH = A_HEADS * HEAD_DIM
DILATED_PAIRS = ((128, 1), (512, 4), (2048, 16))
B_WIDTH = 512
CONV_WIDTH = 31
C_Q_HEADS = 16
C_KV_HEADS = 2
C_WIDTH = C_Q_HEADS * HEAD_DIM
C_KV_WIDTH = C_KV_HEADS * HEAD_DIM
C_WINDOW = 128
IN_WIDTH = 3 * A_WIDTH + 2 * B_WIDTH + C_WIDTH + 2 * C_KV_WIDTH
EPS = 1e-6

LANES = 128
ATT_BLOCK = 128
HEAD_CHUNK = 256
HEADS_PER_CHUNK = HEAD_CHUNK // HEAD_DIM
A_QKV_WIDTH = 3 * A_WIDTH
REST_WIDTH = IN_WIDTH - A_QKV_WIDTH
KV_REP_WIDTH = 2 * C_KV_HEADS * HEAD_CHUNK
CONV_HALO = 32
MASKED_SCORE = -1e30
V7X_VMEM_LIMIT = 56 * 1024 * 1024

_bf16 = jnp.bfloat16
_f32 = jnp.float32


def _sigmoid(v):
    return 1.0 / (1.0 + jnp.exp(-v))


def _in_proj_kernel(x_ref, g_ref, w_ref, e_ref, rep_ref, gain_ref, flag_ref,
                    a1_ref, a4_ref, a16_ref, rest_ref, kv_ref, h_ref, slab_ref, *,
                    tm, chunks_per_tile, n_a_tiles, qk_chunks, cq_chunk, ckv_chunk):
    j = pl.program_id(1)

    @pl.when(j == 0)
    def _():
        x = x_ref[...]
        ms = jnp.mean(x * x, axis=-1, keepdims=True)
        h_ref[...] = (x * lax.rsqrt(ms + EPS) * g_ref[...]).astype(_bf16)

    def head_norm(a, sl):
        ss = jnp.dot((a * a).astype(_bf16), e_ref[...], preferred_element_type=_f32)
        yn = a * lax.rsqrt(ss * (1.0 / HEAD_DIM) + EPS) * gain_ref[:, sl]
        return jnp.where(flag_ref[:, sl] > 0.0, yn, a)

    def emit_a(val, c):
        sl = slice(c * HEAD_CHUNK, (c + 1) * HEAD_CHUNK)
        a1_ref[:, sl] = val.astype(_bf16)
        for s in range(HEAD_CHUNK // LANES):
            slab_ref[s] = val[:, s * LANES:(s + 1) * LANES]
        for s in range(HEAD_CHUNK // LANES):
            cs = slice(c * HEAD_CHUNK + s * LANES, c * HEAD_CHUNK + (s + 1) * LANES)
            for (_, d), ref in zip(DILATED_PAIRS[1:], (a4_ref, a16_ref)):
                for r in range(d):
                    ref[0, r, :, cs] = slab_ref[s, pl.ds(r, tm // d, stride=d), :].astype(_bf16)

    for c in range(chunks_per_tile):
        sl = slice(c * HEAD_CHUNK, (c + 1) * HEAD_CHUNK)
        a = jnp.dot(h_ref[...], w_ref[:, sl], preferred_element_type=_f32)
        chunk = j * chunks_per_tile + c
        is_a = j < n_a_tiles

        @pl.when(jnp.logical_and(is_a, chunk < qk_chunks))
        def _():
            emit_a(head_norm(a, sl), c)

        @pl.when(jnp.logical_and(is_a, chunk >= qk_chunks))
        def _():
            emit_a(a, c)

        @pl.when(jnp.logical_and(jnp.logical_not(is_a), chunk < cq_chunk))
        def _():
            rest_ref[:, sl] = a.astype(_bf16)

        @pl.when(jnp.logical_and(chunk >= cq_chunk, chunk < ckv_chunk))
        def _():
            rest_ref[:, sl] = head_norm(a, sl).astype(_bf16)

        @pl.when(chunk == ckv_chunk)
        def _():
            kv = head_norm(a, sl).astype(_bf16)
            rest_ref[:, sl] = kv
            kv_ref[...] = jnp.dot(kv, rep_ref[...], preferred_element_type=_f32).astype(_bf16)


def _in_proj(x, g, w, head_ones, kv_rep, gain, flag, *, batch, tm, tn):
    m, d = x.shape
    s = m // batch
    n = w.shape[1]
    assert m % tm == 0 and s % tm == 0 and n % tn == 0 and tn % HEAD_CHUNK == 0 and A_QKV_WIDTH % tn == 0
    n_a_tiles = A_QKV_WIDTH // tn
    tiles_per_seq = s // tm
    kern = functools.partial(
        _in_proj_kernel, tm=tm, chunks_per_tile=tn // HEAD_CHUNK, n_a_tiles=n_a_tiles,
        qk_chunks=2 * A_WIDTH // HEAD_CHUNK,
        cq_chunk=(A_QKV_WIDTH + 2 * B_WIDTH) // HEAD_CHUNK,
        ckv_chunk=(IN_WIDTH - 2 * C_KV_WIDTH) // HEAD_CHUNK)

    def a_col(j):
        return jnp.minimum(j, n_a_tiles - 1)

    def stream_spec(dil):
        return pl.BlockSpec((1, dil, tm // dil, tn),
                            lambda i, j: (i // tiles_per_seq, 0, i % tiles_per_seq, a_col(j)))

    d4, d16 = DILATED_PAIRS[1][1], DILATED_PAIRS[2][1]
    return pl.pallas_call(
        kern,
        out_shape=(jax.ShapeDtypeStruct((m, A_QKV_WIDTH), _bf16),
                   jax.ShapeDtypeStruct((batch, d4, s // d4, A_QKV_WIDTH), _bf16),
                   jax.ShapeDtypeStruct((batch, d16, s // d16, A_QKV_WIDTH), _bf16),
                   jax.ShapeDtypeStruct((m, REST_WIDTH), _bf16),
                   jax.ShapeDtypeStruct((m, KV_REP_WIDTH), _bf16)),
        grid=(m // tm, n // tn),
        in_specs=[
            pl.BlockSpec((tm, d), lambda i, j: (i, 0)),
            pl.BlockSpec((1, d), lambda i, j: (0, 0)),
            pl.BlockSpec((d, tn), lambda i, j: (0, j)),
            pl.BlockSpec((HEAD_CHUNK, HEAD_CHUNK), lambda i, j: (0, 0)),
            pl.BlockSpec((HEAD_CHUNK, KV_REP_WIDTH), lambda i, j: (0, 0)),
            pl.BlockSpec((1, tn), lambda i, j: (0, j)),
            pl.BlockSpec((1, tn), lambda i, j: (0, j)),
        ],
        out_specs=(pl.BlockSpec((tm, tn), lambda i, j: (i, a_col(j))),
                   stream_spec(d4), stream_spec(d16),
                   pl.BlockSpec((tm, tn), lambda i, j: (i, jnp.maximum(j - n_a_tiles, 0))),
                   pl.BlockSpec((tm, KV_REP_WIDTH), lambda i, j: (i, 0))),
        scratch_shapes=[pltpu.VMEM((tm, d), _bf16), pltpu.VMEM((HEAD_CHUNK // LANES, tm, LANES), _f32)],
        compiler_params=pltpu.CompilerParams(
            dimension_semantics=("parallel", "arbitrary"), vmem_limit_bytes=V7X_VMEM_LIMIT),
        name="in_proj",
    )(x, g, w, head_ones, kv_rep, gain, flag)


def _lane_heads():
    return lax.broadcasted_iota(jnp.int32, (1, HEAD_CHUNK), 1) // HEAD_DIM


def _per_head_lanes(cols):
    half = lax.broadcasted_iota(jnp.int32, (1, LANES), 1) < HEAD_DIM
    return jnp.concatenate([jnp.where(half, cols[0], cols[1]), jnp.where(half, cols[2], cols[3])], axis=1)


def _attn_units(units, *, far_key, sinks=None, want_lse=False):
    t = ATT_BLOCK
    lane_head = _lane_heads()
    row = lax.broadcasted_iota(jnp.int32, (t, t), 0)
    col = lax.broadcasted_iota(jnp.int32, (t, t), 1)
    lower = col <= row
    zero_tile = jnp.zeros((t, t), _bf16)

    scores = []
    for (q, kp, kc, vp, vc, has_prev) in units:
        qs = jnp.concatenate([jnp.where(lane_head == h, q, jnp.zeros_like(q)) for h in range(HEADS_PER_CHUNK)],
                             axis=0)
        kcat = jnp.concatenate([kp, kc], axis=0)
        scores.append(lax.dot_general(qs, kcat, (((1,), (1,)), ((), ())), preferred_element_type=_f32))

    stage = []
    for u, (q, kp, kc, vp, vc, has_prev) in enumerate(units):
        if far_key:
            prod = q.astype(_f32) * kp.astype(_f32)
        ms, ls, sds, pcs = [], [], [], []
        for h in range(HEADS_PER_CHUNK):
            s2 = scores[u][h * t:(h + 1) * t]
            s_prev = s2[:, :t]
            if has_prev is not True:
                s_prev = jnp.where(has_prev, s_prev, MASKED_SCORE)
            s = jnp.where(lower, s2[:, t:], s_prev)
            m = jnp.max(s, axis=-1, keepdims=True)
            if far_key:
                half = (lax.broadcasted_iota(jnp.int32, (1, LANES), 1) // HEAD_DIM) == (h % 2)
                ph = prod[:, (h // 2) * LANES:(h // 2 + 1) * LANES]
                sd = jnp.sum(jnp.where(half, ph, 0.0), axis=-1, keepdims=True)
                if has_prev is not True:
                    sd = jnp.where(has_prev, sd, MASKED_SCORE)
                m = jnp.maximum(m, sd)
                sds.append(sd)
            if sinks is not None:
                m = jnp.maximum(m, sinks[u][h])
            p = jnp.exp(s - m)
            ls.append(jnp.sum(p, axis=-1, keepdims=True))
            ms.append(m)
            pb = p.astype(_bf16)
            pcs.append(jnp.where(lower, zero_tile, pb))
            pcs.append(jnp.where(lower, pb, zero_tile))
        m_all = _per_head_lanes(ms)
        l_all = _per_head_lanes(ls)
        extra = None
        if far_key:
            pd = jnp.exp(_per_head_lanes(sds) - m_all)
            l_all = l_all + pd
            extra = pd * vp.astype(_f32)
        if sinks is not None:
            sink_all = _per_head_lanes([jnp.full((1, 1), sinks[u][h], _f32) for h in range(HEADS_PER_CHUNK)])
            l_all = l_all + jnp.exp(sink_all - m_all)
        stage.append((jnp.concatenate(pcs, axis=1), m_all, l_all, extra))

    outs = []
    for u, (q, kp, kc, vp, vc, has_prev) in enumerate(units):
        p_all, m_all, l_all, extra = stage[u]
        vcat = jnp.concatenate([vp, vc], axis=0)
        v_bd = jnp.concatenate([jnp.where(lane_head == h, vcat, jnp.zeros_like(vcat))
                                for h in range(HEADS_PER_CHUNK)], axis=0)
        o = jnp.dot(p_all, v_bd, preferred_element_type=_f32)
        if extra is not None:
            o = o + extra
        o = o * (1.0 / l_all)
        outs.append((o, m_all + jnp.log(l_all) if want_lse else None))
    return outs


def _merge(o_old, lse_old, o_new, lse_new, want_lse):
    diff = lse_new - lse_old
    e = jnp.exp(-jnp.abs(diff))
    den = 1.0 + e
    w_new = jnp.where(diff >= 0.0, 1.0, e) * (1.0 / den)
    o = o_old + w_new * (o_new - o_old)
    lse = jnp.maximum(lse_new, lse_old) + jnp.log(den) if want_lse else None
    return o, lse


def _attn_a_kernel(q1, k1p, k1c, v1p, v1c, q4, k4p, k4c, v4p, v4c, q16, k16p, k16c, v16p, v16c,
                   o_ref, acc_ref, lse_ref, *, d4, d16):
    t = ATT_BLOCK
    n = pl.program_id(1)
    has_prev = n > 0
    n_slabs = HEAD_CHUNK // LANES
    blocks4 = d16 // d4
    group = 4
    blocks1 = d16

    def slab(x, s):
        return x[:, s * LANES:(s + 1) * LANES]

    def body16(rp, carry):
        units = []
        for u in range(2):
            r = 2 * rp + u
            units.append((q16[0, r], k16p[0, r], k16c[0, r], v16p[0, r], v16c[0, r], has_prev))
        outs = _attn_units(units, far_key=True, want_lse=True)
        for u, (o, lse) in enumerate(outs):
            rows = pl.ds(2 * rp + u, t, stride=d16)
            for s in range(n_slabs):
                acc_ref[s, rows, :] = slab(o, s)
                lse_ref[s, rows, :] = slab(lse, s)
        return carry

    lax.fori_loop(0, d16 // 2, body16, 0)

    def body4(r, carry):
        units = []
        for blk in range(blocks4):
            if blk == 0:
                units.append((q4[0, r, 0:t], k4p[0, r], k4c[0, r, 0:t], v4p[0, r], v4c[0, r, 0:t], has_prev))
            else:
                lo, mid, hi = (blk - 1) * t, blk * t, (blk + 1) * t
                units.append((q4[0, r, mid:hi], k4c[0, r, lo:mid], k4c[0, r, mid:hi],
                              v4c[0, r, lo:mid], v4c[0, r, mid:hi], True))
        outs = _attn_units(units, far_key=True, want_lse=True)
        for blk, (o, lse) in enumerate(outs):
            rows = pl.ds(blk * t * d4 + r, t, stride=d4)
            for s in range(n_slabs):
                o_m, lse_m = _merge(acc_ref[s, rows, :], lse_ref[s, rows, :], slab(o, s), slab(lse, s), True)
                acc_ref[s, rows, :] = o_m
                lse_ref[s, rows, :] = lse_m
        return carry

    lax.fori_loop(0, d4, body4, 0)

    def body1(g, carry):
        base = pl.multiple_of(g * (group * t), group * t)
        units = []
        for bl in range(group):
            start = pl.multiple_of(base + bl * t, t)
            if bl == 0:
                before = pl.multiple_of(jnp.maximum(base - t, 0), t)
                first = g == 0
                kp = jnp.where(first, k1p[0], k1c[0, pl.ds(before, t), :])
                vp = jnp.where(first, v1p[0], v1c[0, pl.ds(before, t), :])
                prev_ok = jnp.logical_or(has_prev, g > 0)
            else:
                before = pl.multiple_of(base + (bl - 1) * t, t)
                kp, vp, prev_ok = k1c[0, pl.ds(before, t), :], v1c[0, pl.ds(before, t), :], True
            units.append((q1[0, pl.ds(start, t), :], kp, k1c[0, pl.ds(start, t), :],
                          vp, v1c[0, pl.ds(start, t), :], prev_ok))
        outs = _attn_units(units, far_key=True, want_lse=True)
        for bl, (o, lse) in enumerate(outs):
            rows = pl.ds(pl.multiple_of(base + bl * t, t), t)
            merged = [_merge(acc_ref[s, rows, :], lse_ref[s, rows, :], slab(o, s), slab(lse, s), False)[0]
                      for s in range(n_slabs)]
            o_ref[0, rows, :] = jnp.concatenate(merged, axis=1).astype(o_ref.dtype)
        return carry

    lax.fori_loop(0, blocks1 // group, body1, 0)


def _attn_a(a1, a4, a16):
    b, s, _ = a1.shape
    t = ATT_BLOCK
    (w1, d1), (w4, d4), (w16, d16) = DILATED_PAIRS
    assert d1 == 1 and w1 == t and w4 == t * d4 and w16 == t * d16 and d16 % d4 == 0
    sup = t * d16
    assert s % sup == 0
    n_chunks = A_WIDTH // HEAD_CHUNK
    qc, kc, vc = 0, n_chunks, 2 * n_chunks

    def nat(rows, per_sup, off, prev):
        def imap(bb, n, c):
            blk = jnp.maximum(n * per_sup - 1, 0) if prev else n
            return (bb, blk, off + c)
        return pl.BlockSpec((1, rows, HEAD_CHUNK), imap)

    def strm(d, rows, per_sup, off, prev):
        def imap(bb, n, c):
            blk = jnp.maximum(n * per_sup - 1, 0) if prev else n
            return (bb, 0, blk, off + c)
        return pl.BlockSpec((1, d, rows, HEAD_CHUNK), imap)

    in_specs = [nat(sup, 1, qc, False),
                nat(t, sup // t, kc, True), nat(sup, 1, kc, False),
                nat(t, sup // t, vc, True), nat(sup, 1, vc, False),
                strm(d4, sup // d4, 1, qc, False),
                strm(d4, t, sup // d4 // t, kc, True), strm(d4, sup // d4, 1, kc, False),
                strm(d4, t, sup // d4 // t, vc, True), strm(d4, sup // d4, 1, vc, False),
                strm(d16, t, 1, qc, False),
                strm(d16, t, 1, kc, True), strm(d16, t, 1, kc, False),
                strm(d16, t, 1, vc, True), strm(d16, t, 1, vc, False)]
    return pl.pallas_call(
        functools.partial(_attn_a_kernel, d4=d4, d16=d16),
        out_shape=jax.ShapeDtypeStruct((b, s, A_WIDTH), _bf16),
        grid=(b, s // sup, n_chunks),
        in_specs=in_specs,
        out_specs=pl.BlockSpec((1, sup, HEAD_CHUNK), lambda bb, n, c: (bb, n, c)),
        scratch_shapes=[pltpu.VMEM((HEAD_CHUNK // LANES, sup, LANES), _f32),
                        pltpu.VMEM((HEAD_CHUNK // LANES, sup, LANES), _f32)],
        compiler_params=pltpu.CompilerParams(
            dimension_semantics=("parallel", "parallel", "parallel"), vmem_limit_bytes=V7X_VMEM_LIMIT),
        name="attn_a",
    )(a1, a1, a1, a1, a1, a4, a4, a4, a4, a4, a16, a16, a16, a16, a16).reshape(b * s, A_WIDTH)


def _attn_c_kernel(sink_ref, q_ref, kp_ref, kc_ref, vp_ref, vc_ref, o_ref):
    g = pl.program_id(1)
    has_prev = pl.program_id(2) > 0
    heads_per_group = C_Q_HEADS // C_KV_HEADS
    units, sinks = [], []
    for c in range(heads_per_group // HEADS_PER_CHUNK):
        sl = slice(c * HEAD_CHUNK, (c + 1) * HEAD_CHUNK)
        units.append((q_ref[0, :, sl], kp_ref[0], kc_ref[0], vp_ref[0], vc_ref[0], has_prev))
        sinks.append([sink_ref[g * heads_per_group + c * HEADS_PER_CHUNK + h] for h in range(HEADS_PER_CHUNK)])
    outs = _attn_units(units, far_key=False, sinks=sinks)
    for c, (o, _) in enumerate(outs):
        o_ref[0, :, c * HEAD_CHUNK:(c + 1) * HEAD_CHUNK] = o.astype(o_ref.dtype)


def _attn_c(rest, kv_rep, sinks):
    b, s, _ = rest.shape
    t = ATT_BLOCK
    group_width = C_WIDTH // C_KV_HEADS
    q_off = 2 * B_WIDTH
    assert C_WINDOW == t and s % t == 0 and q_off % group_width == 0
    q_blk = q_off // group_width

    def kv(off, prev):
        def imap(bb, g, i):
            return (bb, jnp.maximum(i - 1, 0) if prev else i, off + g)
        return pl.BlockSpec((1, t, HEAD_CHUNK), imap)

    return pl.pallas_call(
        _attn_c_kernel,
        out_shape=jax.ShapeDtypeStruct((b, s, C_WIDTH), _bf16),
        grid=(b, C_KV_HEADS, s // t),
        in_specs=[
            pl.BlockSpec(memory_space=pltpu.SMEM),
            pl.BlockSpec((1, t, group_width), lambda bb, g, i: (bb, i, q_blk + g)),
            kv(0, True), kv(0, False), kv(C_KV_HEADS, True), kv(C_KV_HEADS, False),
        ],
        out_specs=pl.BlockSpec((1, t, group_width), lambda bb, g, i: (bb, i, g)),
        compiler_params=pltpu.CompilerParams(
            dimension_semantics=("parallel", "parallel", "parallel"), vmem_limit_bytes=V7X_VMEM_LIMIT),
        name="attn_c",
    )(sinks, rest, kv_rep, kv_rep, kv_rep, kv_rep).reshape(b * s, C_WIDTH)


CONV_ROWS = 64


def _conv_kernel(up_ref, gp_ref, uc_ref, gc_ref, w_ref, b_ref, lg_ref, lb_ref, o_ref, hext_ref, y_ref, *, tc):
    i = pl.program_id(1)
    hp = up_ref[0].astype(_f32) * _sigmoid(gp_ref[0].astype(_f32))
    hext_ref[0:CONV_HALO, :] = jnp.where(i > 0, hp, 0.0)
    hext_ref[CONV_HALO:CONV_HALO + tc, :] = uc_ref[0].astype(_f32) * _sigmoid(gc_ref[0].astype(_f32))
    first = CONV_HALO - (CONV_WIDTH - 1)
    for cg in range(B_WIDTH // LANES):
        cs = slice(cg * LANES, (cg + 1) * LANES)
        for rg in range(tc // CONV_ROWS):
            r0 = rg * CONV_ROWS
            acc = jnp.zeros((CONV_ROWS, LANES), _f32)
            for k in range(CONV_WIDTH):
                acc = acc + w_ref[k:k + 1, cs] * hext_ref[first + r0 + k:first + r0 + k + CONV_ROWS, cs]
            y_ref[r0:r0 + CONV_ROWS, cs] = acc + b_ref[:, cs]
    y = y_ref[...]
    mu = jnp.mean(y, axis=-1, keepdims=True)
    yc = y - mu
    var = jnp.mean(yc * yc, axis=-1, keepdims=True)
    z = yc * lax.rsqrt(var + EPS) * lg_ref[...] + lb_ref[...]
    o_ref[0] = (z * _sigmoid(z)).astype(o_ref.dtype)


def _conv(rest, conv_w, conv_b, ln_g, ln_b, *, tc):
    b, s, _ = rest.shape
    assert s % tc == 0 and tc % CONV_HALO == 0 and CONV_HALO >= CONV_WIDTH - 1
    u_blk, g_blk = 0, 1
    halo_per_block = tc // CONV_HALO

    def cur(col):
        return pl.BlockSpec((1, tc, B_WIDTH), lambda bb, i: (bb, i, col))

    def halo(col):
        return pl.BlockSpec((1, CONV_HALO, B_WIDTH),
                            lambda bb, i: (bb, jnp.maximum(i * halo_per_block - 1, 0), col))

    def vec():
        return pl.BlockSpec((1, B_WIDTH), lambda bb, i: (0, 0))

    return pl.pallas_call(
        functools.partial(_conv_kernel, tc=tc),
        out_shape=jax.ShapeDtypeStruct((b, s, B_WIDTH), _bf16),
        grid=(b, s // tc),
        in_specs=[halo(u_blk), halo(g_blk), cur(u_blk), cur(g_blk),
                  pl.BlockSpec((CONV_WIDTH, B_WIDTH), lambda bb, i: (0, 0)), vec(), vec(), vec()],
        out_specs=pl.BlockSpec((1, tc, B_WIDTH), lambda bb, i: (bb, i, 0)),
        scratch_shapes=[pltpu.VMEM((CONV_HALO + tc, B_WIDTH), _f32), pltpu.VMEM((tc, B_WIDTH), _f32)],
        compiler_params=pltpu.CompilerParams(
            dimension_semantics=("parallel", "parallel"), vmem_limit_bytes=V7X_VMEM_LIMIT),
        name="conformer_conv",
    )(rest, rest, rest, rest, conv_w, conv_b, ln_g, ln_b).reshape(b * s, B_WIDTH)


def _out_proj_kernel(oa_ref, ob_ref, oc_ref, x_ref, w_ref, xo_ref):
    acc = jnp.dot(oa_ref[...], w_ref[0:A_WIDTH, :], preferred_element_type=_f32)
    acc = acc + jnp.dot(ob_ref[...], w_ref[A_WIDTH:A_WIDTH + B_WIDTH, :], preferred_element_type=_f32)
    acc = acc + jnp.dot(oc_ref[...], w_ref[A_WIDTH + B_WIDTH:, :], preferred_element_type=_f32)
    xo_ref[...] = x_ref[...] + acc


def _out_proj(oa, ob, oc, x, w, *, tm):
    m, d = x.shape
    assert m % tm == 0

    def rows(width):
        return pl.BlockSpec((tm, width), lambda i: (i, 0))

    return pl.pallas_call(
        _out_proj_kernel,
        out_shape=jax.ShapeDtypeStruct((m, d), _f32),
        grid=(m // tm,),
        in_specs=[rows(A_WIDTH), rows(B_WIDTH), rows(C_WIDTH), rows(d),
                  pl.BlockSpec(w.shape, lambda i: (0, 0))],
        out_specs=rows(d),
        compiler_params=pltpu.CompilerParams(
            dimension_semantics=("parallel",), vmem_limit_bytes=V7X_VMEM_LIMIT),
        name="out_proj",
    )(oa, ob, oc, x, w)


def _ffn_kernel(x_ref, g_ref, wg_ref, wu_ref, wd_ref, o_ref, h_ref):
    f = pl.program_id(1)

    @pl.when(f == 0)
    def _():
        x = x_ref[...]
        ms = jnp.mean(x * x, axis=-1, keepdims=True)
        h_ref[...] = (x * lax.rsqrt(ms + EPS) * g_ref[...]).astype(_bf16)
        o_ref[...] = x

    h = h_ref[...]
    gate = jnp.dot(h, wg_ref[...], preferred_element_type=_f32)
    up = jnp.dot(h, wu_ref[...], preferred_element_type=_f32)
    act = (gate * _sigmoid(gate) * up).astype(_bf16)
    o_ref[...] += jnp.dot(act, wd_ref[...], preferred_element_type=_f32)


def _ffn(x, g, wg, wu, wd, *, tm, tf):
    m, d = x.shape
    ff = wg.shape[1]
    assert m % tm == 0 and ff % tf == 0
    return pl.pallas_call(
        _ffn_kernel,
        out_shape=jax.ShapeDtypeStruct((m, d), _f32),
        grid=(m // tm, ff // tf),
        in_specs=[
            pl.BlockSpec((tm, d), lambda i, f: (i, 0)),
            pl.BlockSpec((1, d), lambda i, f: (0, 0)),
            pl.BlockSpec((d, tf), lambda i, f: (0, f)),
            pl.BlockSpec((d, tf), lambda i, f: (0, f)),
            pl.BlockSpec((tf, d), lambda i, f: (f, 0)),
        ],
        out_specs=pl.BlockSpec((tm, d), lambda i, f: (i, 0)),
        scratch_shapes=[pltpu.VMEM((tm, d), _bf16)],
        compiler_params=pltpu.CompilerParams(
            dimension_semantics=("parallel", "arbitrary"), vmem_limit_bytes=V7X_VMEM_LIMIT),
        name="swiglu_ffn",
    )(x, g, wg, wu, wd)


def _tile(m, pref):
    t = pref
    while m % t:
        t //= 2
    return t


def _qk_gain_and_flag(a_q_g, a_k_g, c_q_g, c_k_g):
    scale = HEAD_DIM ** -0.5
    ones = lambda n: jnp.ones((n,), _f32)
    zeros = lambda n: jnp.zeros((n,), _f32)
    gain = jnp.concatenate([
        jnp.tile(a_q_g * scale, A_HEADS), jnp.tile(a_k_g, A_HEADS), ones(A_WIDTH + 2 * B_WIDTH),
        jnp.tile(c_q_g * scale, C_Q_HEADS), jnp.tile(c_k_g, C_KV_HEADS), ones(C_KV_WIDTH)])
    flag = jnp.concatenate([ones(2 * A_WIDTH), zeros(A_WIDTH + 2 * B_WIDTH),
                            ones(C_WIDTH + C_KV_WIDTH), zeros(C_KV_WIDTH)])
    return gain.reshape(1, IN_WIDTH), flag.reshape(1, IN_WIDTH)


def kernel(x, norm1_g, w_in, a_q_g, a_k_g, conv_w, conv_b, conv_ln_g, conv_ln_b,
           c_q_g, c_k_g, c_sinks, w_out, norm2_g, w_gate, w_up, w_down):
    b, s, d = x.shape
    m = b * s
    depth = w_in.shape[0]
    lane = jnp.arange(HEAD_CHUNK) // HEAD_DIM
    head_ones = (lane[:, None] == lane[None, :]).astype(_bf16)
    dst = jnp.arange(KV_REP_WIDTH)
    kv_rep = (jnp.arange(HEAD_CHUNK)[:, None] ==
              (dst // HEAD_CHUNK) * HEAD_DIM + dst % HEAD_DIM).astype(_bf16)
    xf = x.reshape(m, d)
    for l in range(depth):
        gain, flag = _qk_gain_and_flag(a_q_g[l], a_k_g[l], c_q_g[l], c_k_g[l])
        a1, a4, a16, rest, kvr = _in_proj(
            xf, norm1_g[l].reshape(1, d), w_in[l].astype(_bf16), head_ones, kv_rep, gain, flag,
            batch=b, tm=_tile(s, 1024), tn=768)
        oa = _attn_a(a1.reshape(b, s, A_QKV_WIDTH), a4, a16)
        rest = rest.reshape(b, s, REST_WIDTH)
        ob = _conv(rest, conv_w[l], conv_b[l].reshape(1, -1), conv_ln_g[l].reshape(1, -1),
                   conv_ln_b[l].reshape(1, -1), tc=256)
        oc = _attn_c(rest, kvr.reshape(b, s, KV_REP_WIDTH), c_sinks[l])
        xf = _out_proj(oa, ob, oc, xf, w_out[l].astype(_bf16), tm=_tile(m, 256))
        xf = _ffn(xf, norm2_g[l].reshape(1, d), w_gate[l].astype(_bf16), w_up[l].astype(_bf16),
                  w_down[l].astype(_bf16), tm=_tile(m, 512), tf=512)
    return xf.reshape(b, s, d)
```

```python
import functools

import jax
import jax.numpy as jnp
from jax import lax
from jax.experimental import pallas as pl
from jax.experimental.pallas import tpu as pltpu

HEAD_DIM = 64
A_HEADS = 8
A_WIDTH = A_HEADS * HEAD_DIM
DILATED_PAIRS = ((128, 1), (512, 4), (2048, 16))
B_WIDTH = 512
CONV_WIDTH = 31
C_Q_HEADS = 16
C_KV_HEADS = 2
C_WIDTH = C_Q_HEADS * HEAD_DIM
C_KV_WIDTH = C_KV_HEADS * HEAD_DIM
C_WINDOW = 128
IN_WIDTH = 3 * A_WIDTH + 2 * B_WIDTH + C_WIDTH + 2 * C_KV_WIDTH
EPS = 1e-6

LANES = 128
ATT_BLOCK = 128
HEAD_CHUNK = 256
HEADS_PER_CHUNK = HEAD_CHUNK // HEAD_DIM
N_SLABS = HEAD_CHUNK // LANES
A_QKV_WIDTH = 3 * A_WIDTH
REST_WIDTH = IN_WIDTH - A_QKV_WIDTH
KV_REP_WIDTH = 2 * C_KV_HEADS * HEAD_CHUNK
CONV_HALO = 32
MASKED_SCORE = -1e30
V7X_VMEM_LIMIT = 56 * 1024 * 1024

_bf16 = jnp.bfloat16
_f32 = jnp.float32


def _sigmoid(v):
    return 1.0 / (1.0 + jnp.exp(-v))


def _resident(shape, index_map):
    return pl.BlockSpec(shape, index_map, pipeline_mode=pl.Buffered(1))


def _in_proj_kernel(x_ref, g_ref, w_ref, e_ref, rep_ref, gain_ref, flag_ref,
                    a1_ref, a4_ref, a16_ref, rest_ref, kv_ref, h_ref, slab_ref, *, tm):
    x = x_ref[...]
    ms = jnp.mean(x * x, axis=-1, keepdims=True)
    h_ref[...] = (x * lax.rsqrt(ms + EPS) * g_ref[...]).astype(_bf16)

    n_chunks = IN_WIDTH // HEAD_CHUNK
    a_chunks = A_QKV_WIDTH // HEAD_CHUNK
    qk_chunks = 2 * A_WIDTH // HEAD_CHUNK
    cq_chunk = (A_QKV_WIDTH + 2 * B_WIDTH) // HEAD_CHUNK

    def project(chunk):
        return jnp.dot(h_ref[...], w_ref[0, :, chunk * HEAD_CHUNK:(chunk + 1) * HEAD_CHUNK],
                       preferred_element_type=_f32)

    a_next = project(0)
    for chunk in range(n_chunks):
        sl = slice(chunk * HEAD_CHUNK, (chunk + 1) * HEAD_CHUNK)
        a = a_next
        if chunk + 1 < n_chunks:
            a_next = project(chunk + 1)
        if chunk < qk_chunks or chunk >= cq_chunk:
            ss = jnp.dot((a * a).astype(_bf16), e_ref[...], preferred_element_type=_f32)
            yn = a * lax.rsqrt(ss * (1.0 / HEAD_DIM) + EPS) * gain_ref[:, sl]
            a = jnp.where(flag_ref[:, sl] > 0.0, yn, a) if chunk == n_chunks - 1 else yn
        if chunk < a_chunks:
            a1_ref[:, sl] = a.astype(_bf16)
            base = (chunk % 2) * N_SLABS
            for s in range(N_SLABS):
                slab_ref[base + s] = a[:, s * LANES:(s + 1) * LANES]
            for s in range(N_SLABS):
                cs = slice(chunk * HEAD_CHUNK + s * LANES, chunk * HEAD_CHUNK + (s + 1) * LANES)
                for (_, d), ref in zip(DILATED_PAIRS[1:], (a4_ref, a16_ref)):
                    for r in range(d):
                        ref[0, r, :, cs] = slab_ref[base + s, pl.ds(r, tm // d, stride=d), :].astype(_bf16)
        else:
            rsl = slice(sl.start - A_QKV_WIDTH, sl.stop - A_QKV_WIDTH)
            ab = a.astype(_bf16)
            rest_ref[:, rsl] = ab
            if chunk == n_chunks - 1:
                kv_ref[...] = jnp.dot(ab, rep_ref[...], preferred_element_type=_f32).astype(_bf16)


def _in_proj(x, g, w, layer, head_ones, kv_rep, gain, flag, *, batch, tm):
    m, d = x.shape
    s = m // batch
    n = w.shape[2]
    assert n == IN_WIDTH and s % tm == 0 and (IN_WIDTH - 2 * C_KV_WIDTH) % HEAD_CHUNK == 0
    tiles_per_seq = s // tm

    def stream_spec(dil):
        return pl.BlockSpec((1, dil, tm // dil, A_QKV_WIDTH),
                            lambda i: (i // tiles_per_seq, 0, i % tiles_per_seq, 0))

    def rows(width):
        return pl.BlockSpec((tm, width), lambda i: (i, 0))

    d4, d16 = DILATED_PAIRS[1][1], DILATED_PAIRS[2][1]
    return pl.pallas_call(
        functools.partial(_in_proj_kernel, tm=tm),
        out_shape=(jax.ShapeDtypeStruct((m, A_QKV_WIDTH), _bf16),
                   jax.ShapeDtypeStruct((batch, d4, s // d4, A_QKV_WIDTH), _bf16),
                   jax.ShapeDtypeStruct((batch, d16, s // d16, A_QKV_WIDTH), _bf16),
                   jax.ShapeDtypeStruct((m, REST_WIDTH), _bf16),
                   jax.ShapeDtypeStruct((m, KV_REP_WIDTH), _bf16)),
        grid=(m // tm,),
        in_specs=[
            rows(d),
            _resident((1, d), lambda i: (0, 0)),
            _resident((1, d, n), lambda i: (layer, 0, 0)),
            _resident((HEAD_CHUNK, HEAD_CHUNK), lambda i: (0, 0)),
            _resident((HEAD_CHUNK, KV_REP_WIDTH), lambda i: (0, 0)),
            _resident((1, n), lambda i: (0, 0)),
            _resident((1, n), lambda i: (0, 0)),
        ],
        out_specs=(rows(A_QKV_WIDTH), stream_spec(d4), stream_spec(d16), rows(REST_WIDTH), rows(KV_REP_WIDTH)),
        scratch_shapes=[pltpu.VMEM((tm, d), _bf16), pltpu.VMEM((2 * N_SLABS, tm, LANES), _f32)],
        compiler_params=pltpu.CompilerParams(
            dimension_semantics=("parallel",), vmem_limit_bytes=V7X_VMEM_LIMIT),
        name="in_proj",
    )(x, g, w, head_ones, kv_rep, gain, flag)


def _per_head_lanes(cols):
    half = lax.broadcasted_iota(jnp.int32, (1, LANES), 1) < HEAD_DIM
    return jnp.concatenate([jnp.where(half, cols[0], cols[1]), jnp.where(half, cols[2], cols[3])], axis=1)


def _attn_units(units, *, head_ones=None, sinks=None):
    t = ATT_BLOCK
    lane_head = lax.broadcasted_iota(jnp.int32, (1, HEAD_CHUNK), 1) // HEAD_DIM
    head_mask = [(lane_head == h).astype(_bf16) for h in range(HEADS_PER_CHUNK)]
    row = lax.broadcasted_iota(jnp.int32, (t, t), 0)
    col = lax.broadcasted_iota(jnp.int32, (t, t), 1)
    lower = col <= row
    lower_bf = lower.astype(_bf16)
    far_key = head_ones is not None

    scores = []
    for (q, kp, kc, vp, vc, has_prev) in units:
        n = q.shape[1] // HEAD_CHUNK
        qs = jnp.concatenate([q[:, c * HEAD_CHUNK:(c + 1) * HEAD_CHUNK] * head_mask[h]
                              for c in range(n) for h in range(HEADS_PER_CHUNK)], axis=0)
        kcat = jnp.concatenate([kp, kc], axis=0)
        scores.append(lax.dot_general(qs, kcat, (((1,), (1,)), ((), ())), preferred_element_type=_f32))
    far_scores = []
    if far_key:
        for (q, kp, kc, vp, vc, has_prev) in units:
            far_scores.append(jnp.dot(q * kp, head_ones, preferred_element_type=_f32))

    stage = []
    for u, (q, kp, kc, vp, vc, has_prev) in enumerate(units):
        n = q.shape[1] // HEAD_CHUNK
        per_chunk = []
        for c in range(n):
            ms, ls, pcs = [], [], []
            for h in range(HEADS_PER_CHUNK):
                base = (c * HEADS_PER_CHUNK + h) * t
                s2 = scores[u][base:base + t]
                s_prev = s2[:, :t]
                if has_prev is not True:
                    s_prev = jnp.where(has_prev, s_prev, MASKED_SCORE)
                s = jnp.where(lower, s2[:, t:], s_prev)
                m = jnp.max(s, axis=-1, keepdims=True)
                p = jnp.exp(s - m)
                ls.append(jnp.sum(p, axis=-1, keepdims=True))
                ms.append(m)
                pb = p.astype(_bf16)
                p_cur = pb * lower_bf
                pcs.append(pb - p_cur)
                pcs.append(p_cur)
            m_all = _per_head_lanes(ms)
            l_all = _per_head_lanes(ls)
            rescale = extra_w = None
            if far_key or sinks is not None:
                if far_key:
                    es = far_scores[u]
                    if has_prev is not True:
                        es = jnp.where(has_prev, es, MASKED_SCORE)
                else:
                    es = sinks[u][c]
                m_new = jnp.maximum(m_all, es)
                rescale = jnp.exp(m_all - m_new)
                extra_w = jnp.exp(es - m_new)
                l_all = l_all * rescale + extra_w
                m_all = m_new
            per_chunk.append((jnp.concatenate(pcs, axis=1), m_all, l_all, rescale, extra_w))
        stage.append(per_chunk)

    outs = []
    for u, (q, kp, kc, vp, vc, has_prev) in enumerate(units):
        vcat = jnp.concatenate([vp, vc], axis=0)
        v_bd = jnp.concatenate([vcat * head_mask[h] for h in range(HEADS_PER_CHUNK)], axis=0)
        res = []
        for (p_all, m_all, l_all, rescale, extra_w) in stage[u]:
            acc = jnp.dot(p_all, v_bd, preferred_element_type=_f32)
            if rescale is not None:
                acc = acc * rescale
            if far_key:
                acc = acc + extra_w * vp.astype(_f32)
            res.append((acc, m_all, l_all))
        outs.append(res)
    return outs


def _merge(old, new):
    acc_o, m_o, l_o = old
    acc_n, m_n, l_n = new
    m = jnp.maximum(m_o, m_n)
    w_o = jnp.exp(m_o - m)
    w_n = jnp.exp(m_n - m)
    return w_o * acc_o + w_n * acc_n, m, w_o * l_o + w_n * l_n


def _attn_a_kernel(e_ref, q1, k1p, k1c, v1p, v1c, q4, k4p, k4c, v4p, v4c, q16, k16p, k16c, v16p, v16c,
                   o_ref, acc_ref, m_ref, l_ref, *, d4, d16):
    t = ATT_BLOCK
    has_prev = pl.program_id(1) > 0
    group = 4
    blocks4 = d16 // d4
    blocks1 = d16
    state = (acc_ref, m_ref, l_ref)
    head_ones = e_ref[...]

    def slab(x, s):
        return x[:, s * LANES:(s + 1) * LANES]

    def load_state(rows):
        return tuple(jnp.concatenate([ref[s, rows, :] for s in range(N_SLABS)], axis=1) for ref in state)

    def store_state(rows, vals):
        for ref, val in zip(state, vals):
            for s in range(N_SLABS):
                ref[s, rows, :] = slab(val, s)

    group16 = 2

    def body16(rg, carry):
        rs = [group16 * rg + u for u in range(group16)]
        units = [(q16[0, r], k16p[0, r], k16c[0, r], v16p[0, r], v16c[0, r], has_prev) for r in rs]
        outs = _attn_units(units, head_ones=head_ones)
        for r, res in zip(rs, outs):
            store_state(pl.ds(r, t, stride=d16), res[0])
        return carry

    lax.fori_loop(0, d16 // group16, body16, 0)

    def body4(r, carry):
        units = []
        for blk in range(blocks4):
            if blk == 0:
                units.append((q4[0, r, 0:t], k4p[0, r], k4c[0, r, 0:t], v4p[0, r], v4c[0, r, 0:t], has_prev))
            else:
                lo, mid, hi = (blk - 1) * t, blk * t, (blk + 1) * t
                units.append((q4[0, r, mid:hi], k4c[0, r, lo:mid], k4c[0, r, mid:hi],
                              v4c[0, r, lo:mid], v4c[0, r, mid:hi], True))
        outs = _attn_units(units, head_ones=head_ones)
        for blk, res in enumerate(outs):
            rows = pl.ds(blk * t * d4 + r, t, stride=d4)
            store_state(rows, _merge(load_state(rows), res[0]))
        return carry

    lax.fori_loop(0, d4, body4, 0)

    def body1(g, carry):
        base = pl.multiple_of(g * (group * t), group * t)
        units = []
        for bl in range(group):
            start = pl.multiple_of(base + bl * t, t)
            if bl == 0:
                before = pl.multiple_of(jnp.maximum(base - t, 0), t)
                first = g == 0
                kp = jnp.where(first, k1p[0], k1c[0, pl.ds(before, t), :])
                vp = jnp.where(first, v1p[0], v1c[0, pl.ds(before, t), :])
                prev_ok = jnp.logical_or(has_prev, g > 0)
            else:
                before = pl.multiple_of(base + (bl - 1) * t, t)
                kp, vp, prev_ok = k1c[0, pl.ds(before, t), :], v1c[0, pl.ds(before, t), :], True
            units.append((q1[0, pl.ds(start, t), :], kp, k1c[0, pl.ds(start, t), :],
                          vp, v1c[0, pl.ds(start, t), :], prev_ok))
        outs = _attn_units(units, head_ones=head_ones)
        for bl, res in enumerate(outs):
            rows = pl.ds(pl.multiple_of(base + bl * t, t), t)
            acc, _, l = _merge(load_state(rows), res[0])
            o_ref[0, rows, :] = (acc * (1.0 / l)).astype(o_ref.dtype)
        return carry

    lax.fori_loop(0, blocks1 // group, body1, 0)


def _attn_a(a1, a4, a16, head_ones):
    b, s, _ = a1.shape
    t = ATT_BLOCK
    (w1, d1), (w4, d4), (w16, d16) = DILATED_PAIRS
    assert d1 == 1 and w1 == t and w4 == t * d4 and w16 == t * d16 and d16 % d4 == 0
    sup = t * d16
    assert s % sup == 0
    n_chunks = A_WIDTH // HEAD_CHUNK
    qc, kc, vc = 0, n_chunks, 2 * n_chunks

    def nat(rows, per_sup, off, prev):
        def imap(bb, n, c):
            blk = jnp.maximum(n * per_sup - 1, 0) if prev else n
            return (bb, blk, off + c)
        return pl.BlockSpec((1, rows, HEAD_CHUNK), imap)

    def strm(d, rows, per_sup, off, prev):
        def imap(bb, n, c):
            blk = jnp.maximum(n * per_sup - 1, 0) if prev else n
            return (bb, 0, blk, off + c)
        return pl.BlockSpec((1, d, rows, HEAD_CHUNK), imap)

    in_specs = [_resident((HEAD_CHUNK, HEAD_CHUNK), lambda bb, n, c: (0, 0)),
                nat(sup, 1, qc, False),
                nat(t, sup // t, kc, True), nat(sup, 1, kc, False),
                nat(t, sup // t, vc, True), nat(sup, 1, vc, False),
                strm(d4, sup // d4, 1, qc, False),
                strm(d4, t, sup // d4 // t, kc, True), strm(d4, sup // d4, 1, kc, False),
                strm(d4, t, sup // d4 // t, vc, True), strm(d4, sup // d4, 1, vc, False),
                strm(d16, t, 1, qc, False),
                strm(d16, t, 1, kc, True), strm(d16, t, 1, kc, False),
                strm(d16, t, 1, vc, True), strm(d16, t, 1, vc, False)]
    state = pltpu.VMEM((N_SLABS, sup, LANES), _f32)
    return pl.pallas_call(
        functools.partial(_attn_a_kernel, d4=d4, d16=d16),
        out_shape=jax.ShapeDtypeStruct((b, s, A_WIDTH), _bf16),
        grid=(b, s // sup, n_chunks),
        in_specs=in_specs,
        out_specs=pl.BlockSpec((1, sup, HEAD_CHUNK), lambda bb, n, c: (bb, n, c)),
        scratch_shapes=[state, state, state],
        compiler_params=pltpu.CompilerParams(
            dimension_semantics=("parallel", "parallel", "parallel"), vmem_limit_bytes=V7X_VMEM_LIMIT),
        name="attn_a",
    )(head_ones, a1, a1, a1, a1, a1, a4, a4, a4, a4, a4, a16, a16, a16, a16, a16).reshape(b * s, A_WIDTH)


def _attn_c_kernel(sink_ref, q_ref, kp_ref, kc_ref, vp_ref, vc_ref, o_ref, *, blocks):
    t = ATT_BLOCK
    g = pl.program_id(1)
    has_prev = pl.program_id(2) > 0
    group_width = C_WIDTH // C_KV_HEADS
    n = group_width // HEAD_CHUNK
    sink_rows = [sink_ref[g * n + c] for c in range(n)]
    units = []
    for blk in range(blocks):
        cur = slice(blk * t, (blk + 1) * t)
        if blk == 0:
            kp, vp, ok = kp_ref[0], vp_ref[0], has_prev
        else:
            before = slice((blk - 1) * t, blk * t)
            kp, vp, ok = kc_ref[0, before], vc_ref[0, before], True
        units.append((q_ref[0, cur], kp, kc_ref[0, cur], vp, vc_ref[0, cur], ok))
    outs = _attn_units(units, sinks=[sink_rows] * blocks)
    for blk, res in enumerate(outs):
        for c, (acc, _, l) in enumerate(res):
            o_ref[0, blk * t:(blk + 1) * t, c * HEAD_CHUNK:(c + 1) * HEAD_CHUNK] = (
                acc * (1.0 / l)).astype(o_ref.dtype)


def _attn_c(rest, kv_rep, sink_lanes, *, blocks):
    b, s, _ = rest.shape
    t = ATT_BLOCK
    rows = blocks * t
    group_width = C_WIDTH // C_KV_HEADS
    q_off = 2 * B_WIDTH
    assert C_WINDOW == t and s % rows == 0 and q_off % group_width == 0
    q_blk = q_off // group_width

    def kv(off, prev):
        if prev:
            return pl.BlockSpec((1, t, HEAD_CHUNK), lambda bb, g, i: (bb, jnp.maximum(i * blocks - 1, 0), off + g))
        return pl.BlockSpec((1, rows, HEAD_CHUNK), lambda bb, g, i: (bb, i, off + g))

    return pl.pallas_call(
        functools.partial(_attn_c_kernel, blocks=blocks),
        out_shape=jax.ShapeDtypeStruct((b, s, C_WIDTH), _bf16),
        grid=(b, C_KV_HEADS, s // rows),
        in_specs=[
            _resident(sink_lanes.shape, lambda bb, g, i: (0, 0, 0)),
            pl.BlockSpec((1, rows, group_width), lambda bb, g, i: (bb, i, q_blk + g)),
            kv(0, True), kv(0, False), kv(C_KV_HEADS, True), kv(C_KV_HEADS, False),
        ],
        out_specs=pl.BlockSpec((1, rows, group_width), lambda bb, g, i: (bb, i, g)),
        compiler_params=pltpu.CompilerParams(
            dimension_semantics=("parallel", "parallel", "parallel"), vmem_limit_bytes=V7X_VMEM_LIMIT),
        name="attn_c",
    )(sink_lanes, rest, kv_rep, kv_rep, kv_rep, kv_rep).reshape(b * s, C_WIDTH)


CONV_ROWS = 64


def _conv_kernel(up_ref, gp_ref, uc_ref, gc_ref, w_ref, b_ref, lg_ref, lb_ref, o_ref, hext_ref, y_ref, *, tc):
    i = pl.program_id(1)
    hp = up_ref[0].astype(_f32) * _sigmoid(gp_ref[0].astype(_f32))
    hext_ref[0:CONV_HALO, :] = jnp.where(i > 0, hp, 0.0)
    hext_ref[CONV_HALO:CONV_HALO + tc, :] = uc_ref[0].astype(_f32) * _sigmoid(gc_ref[0].astype(_f32))
    first = CONV_HALO - (CONV_WIDTH - 1)
    for cg in range(B_WIDTH // LANES):
        cs = slice(cg * LANES, (cg + 1) * LANES)
        for rg in range(tc // CONV_ROWS):
            r0 = rg * CONV_ROWS
            acc = jnp.zeros((CONV_ROWS, LANES), _f32)
            for k in range(CONV_WIDTH):
                acc = acc + w_ref[k:k + 1, cs] * hext_ref[first + r0 + k:first + r0 + k + CONV_ROWS, cs]
            y_ref[r0:r0 + CONV_ROWS, cs] = acc + b_ref[:, cs]
    y = y_ref[...]
    mu = jnp.mean(y, axis=-1, keepdims=True)
    yc = y - mu
    var = jnp.mean(yc * yc, axis=-1, keepdims=True)
    z = yc * lax.rsqrt(var + EPS) * lg_ref[...] + lb_ref[...]
    o_ref[0] = (z * _sigmoid(z)).astype(o_ref.dtype)


def _conv(rest, conv_w, conv_b, ln_g, ln_b, *, tc):
    b, s, _ = rest.shape
    assert s % tc == 0 and tc % CONV_HALO == 0 and CONV_HALO >= CONV_WIDTH - 1
    u_blk, g_blk = 0, 1
    halo_per_block = tc // CONV_HALO

    def cur(col):
        return pl.BlockSpec((1, tc, B_WIDTH), lambda bb, i: (bb, i, col))

    def halo(col):
        return pl.BlockSpec((1, CONV_HALO, B_WIDTH),
                            lambda bb, i: (bb, jnp.maximum(i * halo_per_block - 1, 0), col))

    def vec():
        return pl.BlockSpec((1, B_WIDTH), lambda bb, i: (0, 0))

    return pl.pallas_call(
        functools.partial(_conv_kernel, tc=tc),
        out_shape=jax.ShapeDtypeStruct((b, s, B_WIDTH), _bf16),
        grid=(b, s // tc),
        in_specs=[halo(u_blk), halo(g_blk), cur(u_blk), cur(g_blk),
                  pl.BlockSpec((CONV_WIDTH, B_WIDTH), lambda bb, i: (0, 0)), vec(), vec(), vec()],
        out_specs=pl.BlockSpec((1, tc, B_WIDTH), lambda bb, i: (bb, i, 0)),
        scratch_shapes=[pltpu.VMEM((CONV_HALO + tc, B_WIDTH), _f32), pltpu.VMEM((tc, B_WIDTH), _f32)],
        compiler_params=pltpu.CompilerParams(
            dimension_semantics=("parallel", "parallel"), vmem_limit_bytes=V7X_VMEM_LIMIT),
        name="conformer_conv",
    )(rest, rest, rest, rest, conv_w, conv_b, ln_g, ln_b).reshape(b * s, B_WIDTH)


def _out_proj_kernel(oa_ref, ob_ref, oc_ref, x_ref, w_ref, xo_ref):
    acc = jnp.dot(oa_ref[...], w_ref[0, 0:A_WIDTH, :], preferred_element_type=_f32)
    acc = acc + jnp.dot(ob_ref[...], w_ref[0, A_WIDTH:A_WIDTH + B_WIDTH, :], preferred_element_type=_f32)
    acc = acc + jnp.dot(oc_ref[...], w_ref[0, A_WIDTH + B_WIDTH:, :], preferred_element_type=_f32)
    xo_ref[...] = x_ref[...] + acc


def _out_proj(oa, ob, oc, x, w, layer, *, tm):
    m, d = x.shape
    assert m % tm == 0

    def rows(width):
        return pl.BlockSpec((tm, width), lambda i: (i, 0))

    return pl.pallas_call(
        _out_proj_kernel,
        out_shape=jax.ShapeDtypeStruct((m, d), _f32),
        grid=(m // tm,),
        in_specs=[rows(A_WIDTH), rows(B_WIDTH), rows(C_WIDTH), rows(d),
                  _resident((1,) + w.shape[1:], lambda i: (layer, 0, 0))],
        out_specs=rows(d),
        compiler_params=pltpu.CompilerParams(
            dimension_semantics=("parallel",), vmem_limit_bytes=V7X_VMEM_LIMIT),
        name="out_proj",
    )(oa, ob, oc, x, w)


def _ffn_kernel(x_ref, g_ref, wg_ref, wu_ref, wd_ref, o_ref, h_ref):
    f = pl.program_id(1)

    @pl.when(f == 0)
    def _():
        x = x_ref[...]
        ms = jnp.mean(x * x, axis=-1, keepdims=True)
        h_ref[...] = (x * lax.rsqrt(ms + EPS) * g_ref[...]).astype(_bf16)
        o_ref[...] = x

    h = h_ref[...]
    gate = jnp.dot(h, wg_ref[0], preferred_element_type=_f32)
    up = jnp.dot(h, wu_ref[0], preferred_element_type=_f32)
    act = (gate * _sigmoid(gate) * up).astype(_bf16)
    o_ref[...] += jnp.dot(act, wd_ref[0], preferred_element_type=_f32)


def _ffn(x, g, wg, wu, wd, layer, *, tm, tf):
    m, d = x.shape
    ff = wg.shape[2]
    assert m % tm == 0 and ff % tf == 0
    return pl.pallas_call(
        _ffn_kernel,
        out_shape=jax.ShapeDtypeStruct((m, d), _f32),
        grid=(m // tm, ff // tf),
        in_specs=[
            pl.BlockSpec((tm, d), lambda i, f: (i, 0)),
            pl.BlockSpec((1, d), lambda i, f: (0, 0)),
            pl.BlockSpec((1, d, tf), lambda i, f: (layer, 0, f)),
            pl.BlockSpec((1, d, tf), lambda i, f: (layer, 0, f)),
            pl.BlockSpec((1, tf, d), lambda i, f: (layer, f, 0)),
        ],
        out_specs=pl.BlockSpec((tm, d), lambda i, f: (i, 0)),
        scratch_shapes=[pltpu.VMEM((tm, d), _bf16)],
        compiler_params=pltpu.CompilerParams(
            dimension_semantics=("parallel", "arbitrary"), vmem_limit_bytes=V7X_VMEM_LIMIT),
        name="swiglu_ffn",
    )(x, g, wg, wu, wd)


def _tile(m, pref):
    t = pref
    while m % t:
        t //= 2
    return t


def _qk_gain_and_flag(a_q_g, a_k_g, c_q_g, c_k_g):
    scale = HEAD_DIM ** -0.5
    ones = lambda n: jnp.ones((n,), _f32)
    zeros = lambda n: jnp.zeros((n,), _f32)
    gain = jnp.concatenate([
        jnp.tile(a_q_g * scale, A_HEADS), jnp.tile(a_k_g, A_HEADS), ones(A_WIDTH + 2 * B_WIDTH),
        jnp.tile(c_q_g * scale, C_Q_HEADS), jnp.tile(c_k_g, C_KV_HEADS), ones(C_KV_WIDTH)])
    flag = jnp.concatenate([ones(2 * A_WIDTH), zeros(A_WIDTH + 2 * B_WIDTH),
                            ones(C_WIDTH + C_KV_WIDTH), zeros(C_KV_WIDTH)])
    return gain.reshape(1, IN_WIDTH), flag.reshape(1, IN_WIDTH)


def kernel(x, norm1_g, w_in, a_q_g, a_k_g, conv_w, conv_b, conv_ln_g, conv_ln_b,
           c_q_g, c_k_g, c_sinks, w_out, norm2_g, w_gate, w_up, w_down):
    b, s, d = x.shape
    m = b * s
    depth = w_in.shape[0]
    lane = jnp.arange(HEAD_CHUNK) // HEAD_DIM
    head_ones = (lane[:, None] == lane[None, :]).astype(_bf16)
    dst = jnp.arange(KV_REP_WIDTH)
    kv_rep = (jnp.arange(HEAD_CHUNK)[:, None] ==
              (dst // HEAD_CHUNK) * HEAD_DIM + dst % HEAD_DIM).astype(_bf16)
    w_in, w_out, w_gate, w_up, w_down = (w.astype(_bf16) for w in (w_in, w_out, w_gate, w_up, w_down))
    xf = x.reshape(m, d)
    for l in range(depth):
        gain, flag = _qk_gain_and_flag(a_q_g[l], a_k_g[l], c_q_g[l], c_k_g[l])
        a1, a4, a16, rest, kvr = _in_proj(
            xf, norm1_g[l].reshape(1, d), w_in, l, head_ones, kv_rep, gain, flag, batch=b, tm=_tile(s, 512))
        oa = _attn_a(a1.reshape(b, s, A_QKV_WIDTH), a4, a16, head_ones)
        rest = rest.reshape(b, s, REST_WIDTH)
        ob = _conv(rest, conv_w[l], conv_b[l].reshape(1, -1), conv_ln_g[l].reshape(1, -1),
                   conv_ln_b[l].reshape(1, -1), tc=256)
        sink_lanes = jnp.repeat(c_sinks[l], HEAD_DIM).reshape(C_WIDTH // HEAD_CHUNK, 1, HEAD_CHUNK)
        oc = _attn_c(rest, kvr.reshape(b, s, KV_REP_WIDTH), sink_lanes, blocks=2)
        xf = _out_proj(oa, ob, oc, xf, w_out, l, tm=_tile(m, 256))
        xf = _ffn(xf, norm2_g[l].reshape(1, d), w_gate, w_up, w_down, l, tm=_tile(m, 1024), tf=256)
    return xf.reshape(b, s, d)
```

```python
import functools
import math

import jax
import jax.numpy as jnp
from jax import lax
from jax.experimental import pallas as pl
from jax.experimental.pallas import tpu as pltpu

HEAD_DIM = 64
A_HEADS = 8
A_WIDTH = A_HEADS * HEAD_DIM
DILATED_PAIRS = ((128, 1), (512, 4), (2048, 16))
B_WIDTH = 512
CONV_WIDTH = 31
C_Q_HEADS = 16
C_KV_HEADS = 2
C_WIDTH = C_Q_HEADS * HEAD_DIM
C_KV_WIDTH = C_KV_HEADS * HEAD_DIM
C_WINDOW = 128
IN_WIDTH = 3 * A_WIDTH + 2 * B_WIDTH + C_WIDTH + 2 * C_KV_WIDTH
EPS = 1e-6
LOG2E = math.log2(math.e)

LANES = 128
SUBLANES = 8
ATT_BLOCK = 128
HEAD_CHUNK = 256
HEADS_PER_CHUNK = HEAD_CHUNK // HEAD_DIM
N_SLABS = HEAD_CHUNK // LANES
A_QKV_WIDTH = 3 * A_WIDTH
REST_WIDTH = IN_WIDTH - A_QKV_WIDTH
KV_REP_WIDTH = 2 * C_KV_HEADS * HEAD_CHUNK
CONV_HALO = 32
MASKED_SCORE = -1e30
V7X_VMEM_LIMIT = 56 * 1024 * 1024

_bf16 = jnp.bfloat16
_f32 = jnp.float32


def _sigmoid(v):
    return 1.0 / (1.0 + jnp.exp(-v))


def _resident(shape, index_map):
    return pl.BlockSpec(shape, index_map, pipeline_mode=pl.Buffered(1))


def _in_proj_kernel(x_ref, g_ref, w_ref, e_ref, rep_ref, gain_ref, flag_ref,
                    a1_ref, a4_ref, a16_ref, rest_ref, kv_ref, h_ref, slab_ref, *, tm):
    x = x_ref[...]
    ms = jnp.mean(x * x, axis=-1, keepdims=True)
    h_ref[...] = (x * lax.rsqrt(ms + EPS) * g_ref[...]).astype(_bf16)

    n_chunks = IN_WIDTH // HEAD_CHUNK
    a_chunks = A_QKV_WIDTH // HEAD_CHUNK
    qk_chunks = 2 * A_WIDTH // HEAD_CHUNK
    cq_chunk = (A_QKV_WIDTH + 2 * B_WIDTH) // HEAD_CHUNK

    def project(chunk):
        return jnp.dot(h_ref[...], w_ref[0, :, chunk * HEAD_CHUNK:(chunk + 1) * HEAD_CHUNK],
                       preferred_element_type=_f32)

    a_next = project(0)
    for chunk in range(n_chunks):
        sl = slice(chunk * HEAD_CHUNK, (chunk + 1) * HEAD_CHUNK)
        a = a_next
        if chunk + 1 < n_chunks:
            a_next = project(chunk + 1)
        if chunk < qk_chunks or chunk >= cq_chunk:
            ss = jnp.dot((a * a).astype(_bf16), e_ref[...], preferred_element_type=_f32)
            yn = a * lax.rsqrt(ss * (1.0 / HEAD_DIM) + EPS) * gain_ref[:, sl]
            a = jnp.where(flag_ref[:, sl] > 0.0, yn, a) if chunk == n_chunks - 1 else yn
        if chunk < a_chunks:
            a1_ref[:, sl] = a.astype(_bf16)
            base = (chunk % 2) * N_SLABS
            for s in range(N_SLABS):
                slab_ref[base + s] = a[:, s * LANES:(s + 1) * LANES]
            for s in range(N_SLABS):
                cs = slice(chunk * HEAD_CHUNK + s * LANES, chunk * HEAD_CHUNK + (s + 1) * LANES)
                for (_, d), ref in zip(DILATED_PAIRS[1:], (a4_ref, a16_ref)):
                    for r in range(d):
                        ref[0, r, :, cs] = slab_ref[base + s, pl.ds(r, tm // d, stride=d), :].astype(_bf16)
        else:
            rsl = slice(sl.start - A_QKV_WIDTH, sl.stop - A_QKV_WIDTH)
            ab = a.astype(_bf16)
            rest_ref[:, rsl] = ab
            if chunk == n_chunks - 1:
                kv_ref[...] = jnp.dot(ab, rep_ref[...], preferred_element_type=_f32).astype(_bf16)


def _in_proj(x, g, w, layer, head_ones, kv_rep, gain, flag, *, batch, tm):
    m, d = x.shape
    s = m // batch
    n = w.shape[2]
    assert n == IN_WIDTH and s % tm == 0 and (IN_WIDTH - 2 * C_KV_WIDTH) % HEAD_CHUNK == 0
    tiles_per_seq = s // tm

    def stream_spec(dil):
        return pl.BlockSpec((1, dil, tm // dil, A_QKV_WIDTH),
                            lambda i: (i // tiles_per_seq, 0, i % tiles_per_seq, 0))

    def rows(width):
        return pl.BlockSpec((tm, width), lambda i: (i, 0))

    d4, d16 = DILATED_PAIRS[1][1], DILATED_PAIRS[2][1]
    return pl.pallas_call(
        functools.partial(_in_proj_kernel, tm=tm),
        out_shape=(jax.ShapeDtypeStruct((m, A_QKV_WIDTH), _bf16),
                   jax.ShapeDtypeStruct((batch, d4, s // d4, A_QKV_WIDTH), _bf16),
                   jax.ShapeDtypeStruct((batch, d16, s // d16, A_QKV_WIDTH), _bf16),
                   jax.ShapeDtypeStruct((m, REST_WIDTH), _bf16),
                   jax.ShapeDtypeStruct((m, KV_REP_WIDTH), _bf16)),
        grid=(m // tm,),
        in_specs=[
            rows(d),
            _resident((1, d), lambda i: (0, 0)),
            _resident((1, d, n), lambda i: (layer, 0, 0)),
            _resident((HEAD_CHUNK, HEAD_CHUNK), lambda i: (0, 0)),
            _resident((HEAD_CHUNK, KV_REP_WIDTH), lambda i: (0, 0)),
            _resident((1, n), lambda i: (0, 0)),
            _resident((1, n), lambda i: (0, 0)),
        ],
        out_specs=(rows(A_QKV_WIDTH), stream_spec(d4), stream_spec(d16), rows(REST_WIDTH), rows(KV_REP_WIDTH)),
        scratch_shapes=[pltpu.VMEM((tm, d), _bf16), pltpu.VMEM((2 * N_SLABS, tm, LANES), _f32)],
        compiler_params=pltpu.CompilerParams(
            dimension_semantics=("parallel",), vmem_limit_bytes=V7X_VMEM_LIMIT),
        name="in_proj",
    )(x, g, w, head_ones, kv_rep, gain, flag)


def _per_head_lanes(cols):
    half = lax.broadcasted_iota(jnp.int32, (1, LANES), 1) < HEAD_DIM
    return jnp.concatenate([jnp.where(half, cols[0], cols[1]), jnp.where(half, cols[2], cols[3])], axis=1)


def _attn_consts():
    t = ATT_BLOCK
    lane_head = lax.broadcasted_iota(jnp.int32, (1, HEAD_CHUNK), 1) // HEAD_DIM
    head_mask = [(lane_head == h).astype(_bf16) for h in range(HEADS_PER_CHUNK)]
    lower = lax.broadcasted_iota(jnp.int32, (t, t), 1) <= lax.broadcasted_iota(jnp.int32, (t, t), 0)
    return head_mask, lower, lower.astype(_bf16)


def _attn_scores(units, consts, head_ones):
    head_mask = consts[0]
    scores, far_scores = [], []
    for (q, kp, kc, vp, vc, has_prev) in units:
        n = q.shape[1] // HEAD_CHUNK
        qs = jnp.concatenate([q[:, c * HEAD_CHUNK:(c + 1) * HEAD_CHUNK] * head_mask[h]
                              for c in range(n) for h in range(HEADS_PER_CHUNK)], axis=0)
        kcat = jnp.concatenate([kp, kc], axis=0)
        scores.append(lax.dot_general(qs, kcat, (((1,), (1,)), ((), ())), preferred_element_type=_f32))
    if head_ones is not None:
        for (q, kp, kc, vp, vc, has_prev) in units:
            far_scores.append(jnp.dot(q * kp, head_ones, preferred_element_type=_f32))
    return scores, far_scores


def _attn_finish(units, staged, consts, head_ones, sinks):
    t = ATT_BLOCK
    head_mask, lower, lower_bf = consts
    scores, far_scores = staged
    far_key = head_ones is not None
    stage = []
    for u, (q, kp, kc, vp, vc, has_prev) in enumerate(units):
        per_chunk = []
        for c in range(q.shape[1] // HEAD_CHUNK):
            ms, ls, pcs = [], [], []
            for h in range(HEADS_PER_CHUNK):
                base = (c * HEADS_PER_CHUNK + h) * t
                s2 = scores[u][base:base + t]
                s_prev = s2[:, :t]
                if has_prev is not True:
                    s_prev = jnp.where(has_prev, s_prev, MASKED_SCORE)
                s = jnp.where(lower, s2[:, t:], s_prev)
                m = jnp.max(s, axis=-1, keepdims=True)
                p = jnp.exp2(s - m)
                ls.append(jnp.sum(p, axis=-1, keepdims=True))
                ms.append(m)
                pb = p.astype(_bf16)
                p_cur = pb * lower_bf
                pcs.append(pb - p_cur)
                pcs.append(p_cur)
            m_all = _per_head_lanes(ms)
            l_all = _per_head_lanes(ls)
            rescale = extra_w = None
            if far_key or sinks is not None:
                if far_key:
                    es = far_scores[u]
                    if has_prev is not True:
                        es = jnp.where(has_prev, es, MASKED_SCORE)
                else:
                    es = sinks[u][c]
                m_new = jnp.maximum(m_all, es)
                rescale = jnp.exp2(m_all - m_new)
                extra_w = jnp.exp2(es - m_new)
                l_all = l_all * rescale + extra_w
                m_all = m_new
            per_chunk.append((jnp.concatenate(pcs, axis=1), m_all, l_all, rescale, extra_w))
        stage.append(per_chunk)

    outs = []
    for u, (q, kp, kc, vp, vc, has_prev) in enumerate(units):
        vcat = jnp.concatenate([vp, vc], axis=0)
        v_bd = jnp.concatenate([vcat * head_mask[h] for h in range(HEADS_PER_CHUNK)], axis=0)
        res = []
        for (p_all, m_all, l_all, rescale, extra_w) in stage[u]:
            acc = jnp.dot(p_all, v_bd, preferred_element_type=_f32)
            if rescale is not None:
                acc = acc * rescale
            if far_key:
                acc = acc + extra_w * vp.astype(_f32)
            res.append((acc, m_all, l_all))
        outs.append(res)
    return outs


def _attn_pipeline(groups, *, head_ones=None):
    consts = _attn_consts()
    pending = None
    for grp in list(groups) + [None]:
        nxt = None
        if grp is not None:
            units = grp[0]()
            nxt = (grp, units, _attn_scores(units, consts, head_ones))
        if pending is not None:
            (_, sinks, consume), units_p, staged = pending
            consume(_attn_finish(units_p, staged, consts, head_ones, sinks))
        pending = nxt


def _merge(old, new):
    acc_o, m_o, l_o = old
    acc_n, m_n, l_n = new
    m = jnp.maximum(m_o, m_n)
    w_o = jnp.exp2(m_o - m)
    w_n = jnp.exp2(m_n - m)
    return w_o * acc_o + w_n * acc_n, m, w_o * l_o + w_n * l_n


def _attn_a_kernel(e_ref, q1, k1p, k1c, v1p, v1c, q4, k4p, k4c, v4p, v4c, q16, k16p, k16c, v16p, v16c,
                   o_ref, acc_ref, m_ref, l_ref, *, d4, d16):
    t = ATT_BLOCK
    has_prev = pl.program_id(1) > 0
    group16, group = 2, 4
    blocks4 = d16 // d4
    blocks1 = d16
    state = (acc_ref, m_ref, l_ref)

    def slab(x, s):
        return x[:, s * LANES:(s + 1) * LANES]

    def load_state(rows):
        return tuple(jnp.concatenate([ref[s, rows, :] for s in range(N_SLABS)], axis=1) for ref in state)

    def store_state(rows, vals):
        for ref, val in zip(state, vals):
            for s in range(N_SLABS):
                ref[s, rows, :] = slab(val, s)

    head_ones = e_ref[...]

    def body16(rg, carry):
        rs = [group16 * rg + u for u in range(group16)]

        def make():
            return [(q16[0, r], k16p[0, r], k16c[0, r], v16p[0, r], v16c[0, r], has_prev) for r in rs]

        def consume(outs):
            for r, res in zip(rs, outs):
                store_state(pl.ds(r, t, stride=d16), res[0])

        _attn_pipeline([(make, None, consume)], head_ones=head_ones)
        return carry

    lax.fori_loop(0, d16 // group16, body16, 0)

    def stream_units(base, q, kp0, kc, vp0, vc):
        units = []
        for bl in range(group):
            start = pl.multiple_of(base + bl * t, t)
            if bl == 0:
                before = pl.multiple_of(jnp.maximum(base - t, 0), t)
                first = base == 0
                kp = jnp.where(first, kp0, kc[pl.ds(before, t), :])
                vp = jnp.where(first, vp0, vc[pl.ds(before, t), :])
                ok = jnp.logical_or(has_prev, base > 0)
            else:
                before = pl.multiple_of(base + (bl - 1) * t, t)
                kp, vp, ok = kc[pl.ds(before, t), :], vc[pl.ds(before, t), :], True
            units.append((q[pl.ds(start, t), :], kp, kc[pl.ds(start, t), :], vp, vc[pl.ds(start, t), :], ok))
        return units

    groups4 = blocks4 // group

    def body4(i, carry):
        r = i // groups4
        base = pl.multiple_of((i % groups4) * (group * t), group * t)

        def make():
            return stream_units(base, q4.at[0, r], k4p[0, r], k4c.at[0, r], v4p[0, r], v4c.at[0, r])

        def consume(outs):
            for bl, res in enumerate(outs):
                rows = pl.ds((base + bl * t) * d4 + r, t, stride=d4)
                store_state(rows, _merge(load_state(rows), res[0]))

        _attn_pipeline([(make, None, consume)], head_ones=head_ones)
        return carry

    lax.fori_loop(0, d4 * groups4, body4, 0)

    def body1(g, carry):
        base = pl.multiple_of(g * (group * t), group * t)

        def make():
            return stream_units(base, q1.at[0], k1p[0], k1c.at[0], v1p[0], v1c.at[0])

        def consume(outs):
            for bl, res in enumerate(outs):
                rows = pl.ds(pl.multiple_of(base + bl * t, t), t)
                acc, _, l = _merge(load_state(rows), res[0])
                o_ref[0, rows, :] = (acc * (1.0 / l)).astype(o_ref.dtype)

        _attn_pipeline([(make, None, consume)], head_ones=head_ones)
        return carry

    lax.fori_loop(0, blocks1 // group, body1, 0)


def _attn_a(a1, a4, a16, head_ones):
    b, s, _ = a1.shape
    t = ATT_BLOCK
    (w1, d1), (w4, d4), (w16, d16) = DILATED_PAIRS
    assert d1 == 1 and w1 == t and w4 == t * d4 and w16 == t * d16 and d16 % d4 == 0
    sup = t * d16
    assert s % sup == 0
    n_chunks = A_WIDTH // HEAD_CHUNK
    qc, kc, vc = 0, n_chunks, 2 * n_chunks

    def nat(rows, per_sup, off, prev):
        def imap(bb, n, c):
            blk = jnp.maximum(n * per_sup - 1, 0) if prev else n
            return (bb, blk, off + c)
        return pl.BlockSpec((1, rows, HEAD_CHUNK), imap)

    def strm(d, rows, per_sup, off, prev):
        def imap(bb, n, c):
            blk = jnp.maximum(n * per_sup - 1, 0) if prev else n
            return (bb, 0, blk, off + c)
        return pl.BlockSpec((1, d, rows, HEAD_CHUNK), imap)

    in_specs = [_resident((HEAD_CHUNK, HEAD_CHUNK), lambda bb, n, c: (0, 0)),
                nat(sup, 1, qc, False),
                nat(t, sup // t, kc, True), nat(sup, 1, kc, False),
                nat(t, sup // t, vc, True), nat(sup, 1, vc, False),
                strm(d4, sup // d4, 1, qc, False),
                strm(d4, t, sup // d4 // t, kc, True), strm(d4, sup // d4, 1, kc, False),
                strm(d4, t, sup // d4 // t, vc, True), strm(d4, sup // d4, 1, vc, False),
                strm(d16, t, 1, qc, False),
                strm(d16, t, 1, kc, True), strm(d16, t, 1, kc, False),
                strm(d16, t, 1, vc, True), strm(d16, t, 1, vc, False)]
    state = pltpu.VMEM((N_SLABS, sup, LANES), _f32)
    return pl.pallas_call(
        functools.partial(_attn_a_kernel, d4=d4, d16=d16),
        out_shape=jax.ShapeDtypeStruct((b, s, A_WIDTH), _bf16),
        grid=(b, s // sup, n_chunks),
        in_specs=in_specs,
        out_specs=pl.BlockSpec((1, sup, HEAD_CHUNK), lambda bb, n, c: (bb, n, c)),
        scratch_shapes=[state, state, state],
        compiler_params=pltpu.CompilerParams(
            dimension_semantics=("parallel", "parallel", "parallel"), vmem_limit_bytes=V7X_VMEM_LIMIT),
        name="attn_a",
    )(head_ones, a1, a1, a1, a1, a1, a4, a4, a4, a4, a4, a16, a16, a16, a16, a16).reshape(b * s, A_WIDTH)


def _attn_c_kernel(sink_ref, q_ref, kp_ref, kc_ref, vp_ref, vc_ref, o_ref, *, blocks, group):
    t = ATT_BLOCK
    g = pl.program_id(1)
    has_prev = pl.program_id(2) > 0
    n = C_WIDTH // C_KV_HEADS // HEAD_CHUNK
    sink_rows = [sink_ref[g * n + c] for c in range(n)]
    groups = []
    for b0 in range(0, blocks, group):
        def make(b0=b0):
            units = []
            for blk in range(b0, b0 + group):
                cur = slice(blk * t, (blk + 1) * t)
                if blk == 0:
                    kp, vp, ok = kp_ref[0], vp_ref[0], has_prev
                else:
                    before = slice((blk - 1) * t, blk * t)
                    kp, vp, ok = kc_ref[0, before], vc_ref[0, before], True
                units.append((q_ref[0, cur], kp, kc_ref[0, cur], vp, vc_ref[0, cur], ok))
            return units

        def consume(outs, b0=b0):
            for blk, res in zip(range(b0, b0 + group), outs):
                for c, (acc, _, l) in enumerate(res):
                    o_ref[0, blk * t:(blk + 1) * t, c * HEAD_CHUNK:(c + 1) * HEAD_CHUNK] = (
                        acc * (1.0 / l)).astype(o_ref.dtype)

        groups.append((make, [sink_rows] * group, consume))
    _attn_pipeline(groups)


def _attn_c(rest, kv_rep, sink_lanes, *, blocks, group):
    b, s, _ = rest.shape
    t = ATT_BLOCK
    rows = blocks * t
    group_width = C_WIDTH // C_KV_HEADS
    q_off = 2 * B_WIDTH
    assert C_WINDOW == t and s % rows == 0 and q_off % group_width == 0 and blocks % group == 0
    q_blk = q_off // group_width

    def kv(off, prev):
        if prev:
            return pl.BlockSpec((1, t, HEAD_CHUNK), lambda bb, g, i: (bb, jnp.maximum(i * blocks - 1, 0), off + g))
        return pl.BlockSpec((1, rows, HEAD_CHUNK), lambda bb, g, i: (bb, i, off + g))

    return pl.pallas_call(
        functools.partial(_attn_c_kernel, blocks=blocks, group=group),
        out_shape=jax.ShapeDtypeStruct((b, s, C_WIDTH), _bf16),
        grid=(b, C_KV_HEADS, s // rows),
        in_specs=[
            _resident(sink_lanes.shape, lambda bb, g, i: (0, 0, 0)),
            pl.BlockSpec((1, rows, group_width), lambda bb, g, i: (bb, i, q_blk + g)),
            kv(0, True), kv(0, False), kv(C_KV_HEADS, True), kv(C_KV_HEADS, False),
        ],
        out_specs=pl.BlockSpec((1, rows, group_width), lambda bb, g, i: (bb, i, g)),
        compiler_params=pltpu.CompilerParams(
            dimension_semantics=("parallel", "parallel", "parallel"), vmem_limit_bytes=V7X_VMEM_LIMIT),
        name="attn_c",
    )(sink_lanes, rest, kv_rep, kv_rep, kv_rep, kv_rep).reshape(b * s, C_WIDTH)


CONV_ROWS = 64


def _conv_kernel(up_ref, gp_ref, uc_ref, gc_ref, w_ref, b_ref, lg_ref, lb_ref, o_ref,
                 hext_ref, shift_ref, y_ref, *, tc):
    i = pl.program_id(1)
    hp = up_ref[0].astype(_f32) * _sigmoid(gp_ref[0].astype(_f32))
    hext_ref[0:CONV_HALO, :] = jnp.where(i > 0, hp, 0.0)
    hext_ref[CONV_HALO:CONV_HALO + tc, :] = uc_ref[0].astype(_f32) * _sigmoid(gc_ref[0].astype(_f32))
    shift_rows = tc + CONV_HALO - SUBLANES
    for j in range(1, SUBLANES):
        for cg in range(B_WIDTH // LANES):
            cs = slice(cg * LANES, (cg + 1) * LANES)
            shift_ref[j - 1, :, cs] = hext_ref[j:j + shift_rows, cs]
    first = CONV_HALO - (CONV_WIDTH - 1)
    for cg in range(B_WIDTH // LANES):
        cs = slice(cg * LANES, (cg + 1) * LANES)
        for rg in range(tc // CONV_ROWS):
            r0 = rg * CONV_ROWS
            acc = jnp.zeros((CONV_ROWS, LANES), _f32)
            for k in range(CONV_WIDTH):
                a, j = divmod(first + k, SUBLANES)
                rows = slice(r0 + a * SUBLANES, r0 + a * SUBLANES + CONV_ROWS)
                tap = hext_ref[rows, cs] if j == 0 else shift_ref[j - 1, rows, cs]
                acc = acc + w_ref[k:k + 1, cs] * tap
            y_ref[r0:r0 + CONV_ROWS, cs] = acc + b_ref[:, cs]
    y = y_ref[...]
    mu = jnp.mean(y, axis=-1, keepdims=True)
    yc = y - mu
    var = jnp.mean(yc * yc, axis=-1, keepdims=True)
    z = yc * lax.rsqrt(var + EPS) * lg_ref[...] + lb_ref[...]
    o_ref[0] = (z * _sigmoid(z)).astype(o_ref.dtype)


def _conv(rest, conv_w, conv_b, ln_g, ln_b, *, tc):
    b, s, _ = rest.shape
    assert s % tc == 0 and tc % CONV_HALO == 0 and CONV_HALO >= CONV_WIDTH - 1
    u_blk, g_blk = 0, 1
    halo_per_block = tc // CONV_HALO

    def cur(col):
        return pl.BlockSpec((1, tc, B_WIDTH), lambda bb, i: (bb, i, col))

    def halo(col):
        return pl.BlockSpec((1, CONV_HALO, B_WIDTH),
                            lambda bb, i: (bb, jnp.maximum(i * halo_per_block - 1, 0), col))

    def vec():
        return pl.BlockSpec((1, B_WIDTH), lambda bb, i: (0, 0))

    return pl.pallas_call(
        functools.partial(_conv_kernel, tc=tc),
        out_shape=jax.ShapeDtypeStruct((b, s, B_WIDTH), _bf16),
        grid=(b, s // tc),
        in_specs=[halo(u_blk), halo(g_blk), cur(u_blk), cur(g_blk),
                  pl.BlockSpec((CONV_WIDTH, B_WIDTH), lambda bb, i: (0, 0)), vec(), vec(), vec()],
        out_specs=pl.BlockSpec((1, tc, B_WIDTH), lambda bb, i: (bb, i, 0)),
        scratch_shapes=[pltpu.VMEM((CONV_HALO + tc, B_WIDTH), _f32),
                        pltpu.VMEM((SUBLANES - 1, tc + CONV_HALO - SUBLANES, B_WIDTH), _f32),
                        pltpu.VMEM((tc, B_WIDTH), _f32)],
        compiler_params=pltpu.CompilerParams(
            dimension_semantics=("parallel", "parallel"), vmem_limit_bytes=V7X_VMEM_LIMIT),
        name="conformer_conv",
    )(rest, rest, rest, rest, conv_w, conv_b, ln_g, ln_b).reshape(b * s, B_WIDTH)


def _out_proj_kernel(oa_ref, ob_ref, oc_ref, x_ref, w_ref, xo_ref):
    acc = jnp.dot(oa_ref[...], w_ref[0, 0:A_WIDTH, :], preferred_element_type=_f32)
    acc = acc + jnp.dot(ob_ref[...], w_ref[0, A_WIDTH:A_WIDTH + B_WIDTH, :], preferred_element_type=_f32)
    acc = acc + jnp.dot(oc_ref[...], w_ref[0, A_WIDTH + B_WIDTH:, :], preferred_element_type=_f32)
    xo_ref[...] = x_ref[...] + acc


def _out_proj(oa, ob, oc, x, w, layer, *, tm):
    m, d = x.shape
    assert m % tm == 0

    def rows(width):
        return pl.BlockSpec((tm, width), lambda i: (i, 0))

    return pl.pallas_call(
        _out_proj_kernel,
        out_shape=jax.ShapeDtypeStruct((m, d), _f32),
        grid=(m // tm,),
        in_specs=[rows(A_WIDTH), rows(B_WIDTH), rows(C_WIDTH), rows(d),
                  _resident((1,) + w.shape[1:], lambda i: (layer, 0, 0))],
        out_specs=rows(d),
        compiler_params=pltpu.CompilerParams(
            dimension_semantics=("parallel",), vmem_limit_bytes=V7X_VMEM_LIMIT),
        name="out_proj",
    )(oa, ob, oc, x, w)


def _ffn_kernel(x_ref, g_ref, wg_ref, wu_ref, wd_ref, o_ref, h_ref):
    f = pl.program_id(1)

    @pl.when(f == 0)
    def _():
        x = x_ref[...]
        ms = jnp.mean(x * x, axis=-1, keepdims=True)
        h_ref[...] = (x * lax.rsqrt(ms + EPS) * g_ref[...]).astype(_bf16)
        o_ref[...] = x

    h = h_ref[...]
    gate = jnp.dot(h, wg_ref[0], preferred_element_type=_f32)
    up = jnp.dot(h, wu_ref[0], preferred_element_type=_f32)
    act = (gate * _sigmoid(gate) * up).astype(_bf16)
    o_ref[...] += jnp.dot(act, wd_ref[0], preferred_element_type=_f32)


def _ffn(x, g, wg, wu, wd, layer, *, tm, tf):
    m, d = x.shape
    ff = wg.shape[2]
    assert m % tm == 0 and ff % tf == 0
    return pl.pallas_call(
        _ffn_kernel,
        out_shape=jax.ShapeDtypeStruct((m, d), _f32),
        grid=(m // tm, ff // tf),
        in_specs=[
            pl.BlockSpec((tm, d), lambda i, f: (i, 0)),
            pl.BlockSpec((1, d), lambda i, f: (0, 0)),
            pl.BlockSpec((1, d, tf), lambda i, f: (layer, 0, f)),
            pl.BlockSpec((1, d, tf), lambda i, f: (layer, 0, f)),
            pl.BlockSpec((1, tf, d), lambda i, f: (layer, f, 0)),
        ],
        out_specs=pl.BlockSpec((tm, d), lambda i, f: (i, 0)),
        scratch_shapes=[pltpu.VMEM((tm, d), _bf16)],
        compiler_params=pltpu.CompilerParams(
            dimension_semantics=("parallel", "arbitrary"), vmem_limit_bytes=V7X_VMEM_LIMIT),
        name="swiglu_ffn",
    )(x, g, wg, wu, wd)


def _tile(m, pref):
    t = pref
    while m % t:
        t //= 2
    return t


def _qk_gain_and_flag(a_q_g, a_k_g, c_q_g, c_k_g):
    scale = HEAD_DIM ** -0.5 * LOG2E
    ones = lambda n: jnp.ones((n,), _f32)
    zeros = lambda n: jnp.zeros((n,), _f32)
    gain = jnp.concatenate([
        jnp.tile(a_q_g * scale, A_HEADS), jnp.tile(a_k_g, A_HEADS), ones(A_WIDTH + 2 * B_WIDTH),
        jnp.tile(c_q_g * scale, C_Q_HEADS), jnp.tile(c_k_g, C_KV_HEADS), ones(C_KV_WIDTH)])
    flag = jnp.concatenate([ones(2 * A_WIDTH), zeros(A_WIDTH + 2 * B_WIDTH),
                            ones(C_WIDTH + C_KV_WIDTH), zeros(C_KV_WIDTH)])
    return gain.reshape(1, IN_WIDTH), flag.reshape(1, IN_WIDTH)


def kernel(x, norm1_g, w_in, a_q_g, a_k_g, conv_w, conv_b, conv_ln_g, conv_ln_b,
           c_q_g, c_k_g, c_sinks, w_out, norm2_g, w_gate, w_up, w_down):
    b, s, d = x.shape
    m = b * s
    depth = w_in.shape[0]
    lane = jnp.arange(HEAD_CHUNK) // HEAD_DIM
    head_ones = (lane[:, None] == lane[None, :]).astype(_bf16)
    dst = jnp.arange(KV_REP_WIDTH)
    kv_rep = (jnp.arange(HEAD_CHUNK)[:, None] ==
              (dst // HEAD_CHUNK) * HEAD_DIM + dst % HEAD_DIM).astype(_bf16)
    w_in, w_out, w_gate, w_up, w_down = (w.astype(_bf16) for w in (w_in, w_out, w_gate, w_up, w_down))
    xf = x.reshape(m, d)
    for l in range(depth):
        gain, flag = _qk_gain_and_flag(a_q_g[l], a_k_g[l], c_q_g[l], c_k_g[l])
        a1, a4, a16, rest, kvr = _in_proj(
            xf, norm1_g[l].reshape(1, d), w_in, l, head_ones, kv_rep, gain, flag, batch=b, tm=_tile(s, 512))
        oa = _attn_a(a1.reshape(b, s, A_QKV_WIDTH), a4, a16, head_ones)
        rest = rest.reshape(b, s, REST_WIDTH)
        ob = _conv(rest, conv_w[l], conv_b[l].reshape(1, -1), conv_ln_g[l].reshape(1, -1),
                   conv_ln_b[l].reshape(1, -1), tc=256)
        sink_lanes = jnp.repeat(c_sinks[l] * LOG2E, HEAD_DIM).reshape(C_WIDTH // HEAD_CHUNK, 1, HEAD_CHUNK)
        oc = _attn_c(rest, kvr.reshape(b, s, KV_REP_WIDTH), sink_lanes, blocks=_tile(s // ATT_BLOCK, 4), group=2)
        xf = _out_proj(oa, ob, oc, xf, w_out, l, tm=_tile(m, 512))
        xf = _ffn(xf, norm2_g[l].reshape(1, d), w_gate, w_up, w_down, l, tm=_tile(m, 1024), tf=512)
    return xf.reshape(b, s, d)
```

```python
import functools
import math

import jax
import jax.numpy as jnp
from jax import lax
from jax.experimental import pallas as pl
from jax.experimental.pallas import tpu as pltpu

HEAD_DIM = 64
A_HEADS = 8
A_WIDTH = A_HEADS * HEAD_DIM
DILATED_PAIRS = ((128, 1), (512, 4), (2048, 16))
B_WIDTH = 512
CONV_WIDTH = 31
C_Q_HEADS = 16
C_KV_HEADS = 2
C_WIDTH = C_Q_HEADS * HEAD_DIM
C_KV_WIDTH = C_KV_HEADS * HEAD_DIM
C_WINDOW = 128
IN_WIDTH = 3 * A_WIDTH + 2 * B_WIDTH + C_WIDTH + 2 * C_KV_WIDTH
EPS = 1e-6
LOG2E = math.log2(math.e)

LANES = 128
SUBLANES = 8
ATT_BLOCK = 128
HEAD_CHUNK = 256
HEADS_PER_CHUNK = HEAD_CHUNK // HEAD_DIM
N_SLABS = HEAD_CHUNK // LANES
A_QKV_WIDTH = 3 * A_WIDTH
REST_WIDTH = C_WIDTH + 2 * C_KV_WIDTH
KV_REP_WIDTH = 2 * C_KV_HEADS * HEAD_CHUNK
CONV_HALO = 32
MASKED_SCORE = -1e30
V7X_VMEM_LIMIT = 56 * 1024 * 1024

_bf16 = jnp.bfloat16
_f32 = jnp.float32


def _sigmoid(v):
    return 1.0 / (1.0 + jnp.exp(-v))


def _resident(shape, index_map):
    return pl.BlockSpec(shape, index_map, pipeline_mode=pl.Buffered(1))


CONV_ROWS = 64


def _conv_tasks(h_halves, is_first, w_ref, b_ref, lg_ref, lb_ref, ob_ref, hext_ref, shift_ref, y_ref, *, tm):
    col_groups = [slice(cg * LANES, (cg + 1) * LANES) for cg in range(B_WIDTH // LANES)]
    shift_rows = tm + CONV_HALO - SUBLANES
    first = CONV_HALO - (CONV_WIDTH - 1)

    def fill(pace):
        tail = hext_ref[tm:tm + CONV_HALO, :]
        hext_ref[0:CONV_HALO, :] = jnp.where(is_first, 0.0, tail)
        for half, h in enumerate(h_halves):
            hext_ref[CONV_HALO:CONV_HALO + tm, half * HEAD_CHUNK:(half + 1) * HEAD_CHUNK] = h

    def shift(j, pace):
        for cs in col_groups:
            shift_ref[j - 1, :, cs] = hext_ref[j:j + shift_rows, cs]

    def taps(cs, r0, pace):
        acc = jnp.zeros((CONV_ROWS, LANES), _f32) + pace
        for k in range(CONV_WIDTH):
            a, j = divmod(first + k, SUBLANES)
            rows = slice(r0 + a * SUBLANES, r0 + a * SUBLANES + CONV_ROWS)
            tap = hext_ref[rows, cs] if j == 0 else shift_ref[j - 1, rows, cs]
            acc = acc + w_ref[k:k + 1, cs] * tap
        y_ref[r0:r0 + CONV_ROWS, cs] = acc + b_ref[:, cs]

    def finish(r0, pace):
        y = y_ref[r0:r0 + CONV_ROWS, :]
        mu = jnp.mean(y, axis=-1, keepdims=True)
        yc = y - mu
        var = jnp.mean(yc * yc, axis=-1, keepdims=True)
        z = yc * lax.rsqrt(var + EPS) * lg_ref[...] + lb_ref[...]
        ob_ref[r0:r0 + CONV_ROWS, :] = (z * _sigmoid(z)).astype(ob_ref.dtype)

    tasks = [fill] + [functools.partial(shift, j) for j in range(1, SUBLANES)]
    for r0 in range(0, tm, CONV_ROWS):
        tasks += [functools.partial(taps, cs, r0) for cs in col_groups]
        tasks.append(functools.partial(finish, r0))
    return tasks


def _in_proj_kernel(x_ref, g_ref, w_ref, e_ref, rep_ref, gain_ref, flag_ref, cw_ref, cb_ref, lg_ref, lb_ref,
                    a1_ref, a4_ref, a16_ref, ob_ref, rest_ref, kv_ref,
                    h_ref, slab_ref, hext_ref, shift_ref, y_ref, *, tm, tiles_per_seq):
    x = x_ref[...]
    ms = jnp.mean(x * x, axis=-1, keepdims=True)
    h_ref[...] = (x * lax.rsqrt(ms + EPS) * g_ref[...]).astype(_bf16)

    n_chunks = IN_WIDTH // HEAD_CHUNK
    a_chunks = A_QKV_WIDTH // HEAD_CHUNK
    qk_chunks = 2 * A_WIDTH // HEAD_CHUNK
    conv_chunks = 2 * B_WIDTH // HEAD_CHUNK
    cq_chunk = a_chunks + conv_chunks
    order = list(range(a_chunks, cq_chunk)) + list(range(a_chunks)) + list(range(cq_chunk, n_chunks))

    def project(chunk):
        return jnp.dot(h_ref[...], w_ref[0, :, chunk * HEAD_CHUNK:(chunk + 1) * HEAD_CHUNK],
                       preferred_element_type=_f32)

    conv_in, conv_todo = [], []
    a_next = project(order[0])
    for pos, chunk in enumerate(order):
        sl = slice(chunk * HEAD_CHUNK, (chunk + 1) * HEAD_CHUNK)
        a = a_next
        if pos + 1 < n_chunks:
            a_next = project(order[pos + 1])
        share = -(-len(conv_todo) // max(n_chunks - 1 - pos, 1))
        if share:
            bits = pltpu.bitcast(a[0:SUBLANES, 0:LANES], jnp.int32)
            pace = lax.shift_right_logical(lax.shift_right_logical(bits, 16), 16)[0:1, :].astype(_f32)
            for task in conv_todo[:share]:
                task(pace)
            del conv_todo[:share]
        if chunk < qk_chunks or chunk >= cq_chunk:
            ss = jnp.dot((a * a).astype(_bf16), e_ref[...], preferred_element_type=_f32)
            yn = a * lax.rsqrt(ss * (1.0 / HEAD_DIM) + EPS) * gain_ref[:, sl]
            a = jnp.where(flag_ref[:, sl] > 0.0, yn, a) if chunk == n_chunks - 1 else yn
        if chunk < a_chunks:
            a1_ref[:, sl] = a.astype(_bf16)
            base = (chunk % 2) * N_SLABS
            for s in range(N_SLABS):
                slab_ref[base + s] = a[:, s * LANES:(s + 1) * LANES]
            for s in range(N_SLABS):
                cs = slice(chunk * HEAD_CHUNK + s * LANES, chunk * HEAD_CHUNK + (s + 1) * LANES)
                for (_, d), ref in zip(DILATED_PAIRS[1:], (a4_ref, a16_ref)):
                    for r in range(d):
                        ref[0, r, :, cs] = slab_ref[base + s, pl.ds(r, tm // d, stride=d), :].astype(_bf16)
        elif chunk < cq_chunk:
            conv_in.append(a)
            if len(conv_in) == conv_chunks:
                half = conv_chunks // 2
                glu = [conv_in[c] * _sigmoid(conv_in[half + c]) for c in range(half)]
                conv_todo = _conv_tasks(glu, pl.program_id(0) % tiles_per_seq == 0, cw_ref, cb_ref, lg_ref,
                                        lb_ref, ob_ref, hext_ref, shift_ref, y_ref, tm=tm)
        else:
            rsl = slice(sl.start - cq_chunk * HEAD_CHUNK, sl.stop - cq_chunk * HEAD_CHUNK)
            ab = a.astype(_bf16)
            rest_ref[:, rsl] = ab
            if chunk == n_chunks - 1:
                kv_ref[...] = jnp.dot(ab, rep_ref[...], preferred_element_type=_f32).astype(_bf16)


def _in_proj(x, g, w, layer, head_ones, kv_rep, gain, flag, conv_w, conv_b, ln_g, ln_b, *, batch, tm):
    m, d = x.shape
    s = m // batch
    n = w.shape[2]
    assert n == IN_WIDTH and s % tm == 0 and (IN_WIDTH - 2 * C_KV_WIDTH) % HEAD_CHUNK == 0
    assert tm % CONV_ROWS == 0 and CONV_HALO >= CONV_WIDTH - 1 and CONV_HALO % SUBLANES == 0
    tiles_per_seq = s // tm

    def stream_spec(dil):
        return pl.BlockSpec((1, dil, tm // dil, A_QKV_WIDTH),
                            lambda i: (i // tiles_per_seq, 0, i % tiles_per_seq, 0))

    def rows(width):
        return pl.BlockSpec((tm, width), lambda i: (i, 0))

    def whole(arr):
        return _resident(arr.shape, lambda i: (0,) * arr.ndim)

    d4, d16 = DILATED_PAIRS[1][1], DILATED_PAIRS[2][1]
    return pl.pallas_call(
        functools.partial(_in_proj_kernel, tm=tm, tiles_per_seq=tiles_per_seq),
        out_shape=(jax.ShapeDtypeStruct((m, A_QKV_WIDTH), _bf16),
                   jax.ShapeDtypeStruct((batch, d4, s // d4, A_QKV_WIDTH), _bf16),
                   jax.ShapeDtypeStruct((batch, d16, s // d16, A_QKV_WIDTH), _bf16),
                   jax.ShapeDtypeStruct((m, B_WIDTH), _bf16),
                   jax.ShapeDtypeStruct((m, REST_WIDTH), _bf16),
                   jax.ShapeDtypeStruct((m, KV_REP_WIDTH), _bf16)),
        grid=(m // tm,),
        in_specs=[
            rows(d),
            whole(g),
            _resident((1, d, n), lambda i: (layer, 0, 0)),
            whole(head_ones), whole(kv_rep), whole(gain), whole(flag),
            whole(conv_w), whole(conv_b), whole(ln_g), whole(ln_b),
        ],
        out_specs=(rows(A_QKV_WIDTH), stream_spec(d4), stream_spec(d16), rows(B_WIDTH), rows(REST_WIDTH),
                   rows(KV_REP_WIDTH)),
        scratch_shapes=[pltpu.VMEM((tm, d), _bf16), pltpu.VMEM((2 * N_SLABS, tm, LANES), _f32),
                        pltpu.VMEM((CONV_HALO + tm, B_WIDTH), _f32),
                        pltpu.VMEM((SUBLANES - 1, tm + CONV_HALO - SUBLANES, B_WIDTH), _f32),
                        pltpu.VMEM((tm, B_WIDTH), _f32)],
        compiler_params=pltpu.CompilerParams(
            dimension_semantics=("arbitrary",), vmem_limit_bytes=V7X_VMEM_LIMIT),
        name="in_proj",
    )(x, g, w, head_ones, kv_rep, gain, flag, conv_w, conv_b, ln_g, ln_b)


def _per_head_lanes(cols):
    half = lax.broadcasted_iota(jnp.int32, (1, LANES), 1) < HEAD_DIM
    return jnp.concatenate([jnp.where(half, cols[0], cols[1]), jnp.where(half, cols[2], cols[3])], axis=1)


def _attn_consts():
    t = ATT_BLOCK
    lane_head = lax.broadcasted_iota(jnp.int32, (1, HEAD_CHUNK), 1) // HEAD_DIM
    head_mask = [(lane_head == h).astype(_bf16) for h in range(HEADS_PER_CHUNK)]
    lower = lax.broadcasted_iota(jnp.int32, (t, t), 1) <= lax.broadcasted_iota(jnp.int32, (t, t), 0)
    return head_mask, lower, lower.astype(_bf16)


def _attn_scores(units, consts, head_ones):
    head_mask = consts[0]
    scores, far_scores = [], []
    for (q, kp, kc, vp, vc, has_prev) in units:
        n = q.shape[1] // HEAD_CHUNK
        qs = jnp.concatenate([q[:, c * HEAD_CHUNK:(c + 1) * HEAD_CHUNK] * head_mask[h]
                              for c in range(n) for h in range(HEADS_PER_CHUNK)], axis=0)
        kcat = jnp.concatenate([kp, kc], axis=0)
        scores.append(lax.dot_general(qs, kcat, (((1,), (1,)), ((), ())), preferred_element_type=_f32))
    if head_ones is not None:
        for (q, kp, kc, vp, vc, has_prev) in units:
            far_scores.append(jnp.dot(q * kp, head_ones, preferred_element_type=_f32))
    return scores, far_scores


def _attn_finish(units, staged, consts, head_ones, sinks):
    t = ATT_BLOCK
    head_mask, lower, lower_bf = consts
    scores, far_scores = staged
    far_key = head_ones is not None
    stage = []
    for u, (q, kp, kc, vp, vc, has_prev) in enumerate(units):
        per_chunk = []
        for c in range(q.shape[1] // HEAD_CHUNK):
            ms, ls, pcs = [], [], []
            for h in range(HEADS_PER_CHUNK):
                base = (c * HEADS_PER_CHUNK + h) * t
                s2 = scores[u][base:base + t]
                s_prev = s2[:, :t]
                if has_prev is not True:
                    s_prev = jnp.where(has_prev, s_prev, MASKED_SCORE)
                s = jnp.where(lower, s2[:, t:], s_prev)
                m = jnp.max(s, axis=-1, keepdims=True)
                p = jnp.exp2(s - m)
                ls.append(jnp.sum(p, axis=-1, keepdims=True))
                ms.append(m)
                pb = p.astype(_bf16)
                p_cur = pb * lower_bf
                pcs.append(pb - p_cur)
                pcs.append(p_cur)
            m_all = _per_head_lanes(ms)
            l_all = _per_head_lanes(ls)
            rescale = extra_w = None
            if far_key or sinks is not None:
                if far_key:
                    es = far_scores[u]
                    if has_prev is not True:
                        es = jnp.where(has_prev, es, MASKED_SCORE)
                else:
                    es = sinks[u][c]
                m_new = jnp.maximum(m_all, es)
                rescale = jnp.exp2(m_all - m_new)
                extra_w = jnp.exp2(es - m_new)
                l_all = l_all * rescale + extra_w
                m_all = m_new
            per_chunk.append((jnp.concatenate(pcs, axis=1), m_all, l_all, rescale, extra_w))
        stage.append(per_chunk)

    outs = []
    for u, (q, kp, kc, vp, vc, has_prev) in enumerate(units):
        vcat = jnp.concatenate([vp, vc], axis=0)
        v_bd = jnp.concatenate([vcat * head_mask[h] for h in range(HEADS_PER_CHUNK)], axis=0)
        res = []
        for (p_all, m_all, l_all, rescale, extra_w) in stage[u]:
            acc = jnp.dot(p_all, v_bd, preferred_element_type=_f32)
            if rescale is not None:
                acc = acc * rescale
            if far_key:
                acc = acc + extra_w * vp.astype(_f32)
            res.append((acc, m_all, l_all))
        outs.append(res)
    return outs


def _attn_pipeline(groups, *, head_ones=None):
    consts = _attn_consts()
    pending = None
    for grp in list(groups) + [None]:
        nxt = None
        if grp is not None:
            units = grp[0]()
            nxt = (grp, units, _attn_scores(units, consts, head_ones))
        if pending is not None:
            (_, sinks, consume), units_p, staged = pending
            consume(_attn_finish(units_p, staged, consts, head_ones, sinks))
        pending = nxt


def _merge(old, new):
    acc_o, m_o, l_o = old
    acc_n, m_n, l_n = new
    m = jnp.maximum(m_o, m_n)
    w_o = jnp.exp2(m_o - m)
    w_n = jnp.exp2(m_n - m)
    return w_o * acc_o + w_n * acc_n, m, w_o * l_o + w_n * l_n


def _attn_a_kernel(e_ref, q1, k1p, k1c, v1p, v1c, q4, k4p, k4c, v4p, v4c, q16, k16p, k16c, v16p, v16c,
                   o_ref, acc_ref, m_ref, l_ref, *, d4, d16):
    t = ATT_BLOCK
    has_prev = pl.program_id(1) > 0
    group16, group = 2, 4
    blocks4 = d16 // d4
    blocks1 = d16
    state = (acc_ref, m_ref, l_ref)

    def slab(x, s):
        return x[:, s * LANES:(s + 1) * LANES]

    def load_state(rows):
        return tuple(jnp.concatenate([ref[s, rows, :] for s in range(N_SLABS)], axis=1) for ref in state)

    def store_state(rows, vals):
        for ref, val in zip(state, vals):
            for s in range(N_SLABS):
                ref[s, rows, :] = slab(val, s)

    head_ones = e_ref[...]

    def body16(rg, carry):
        rs = [group16 * rg + u for u in range(group16)]

        def make():
            return [(q16[0, r], k16p[0, r], k16c[0, r], v16p[0, r], v16c[0, r], has_prev) for r in rs]

        def consume(outs):
            for r, res in zip(rs, outs):
                store_state(pl.ds(r, t, stride=d16), res[0])

        _attn_pipeline([(make, None, consume)], head_ones=head_ones)
        return carry

    lax.fori_loop(0, d16 // group16, body16, 0)

    def stream_units(base, q, kp0, kc, vp0, vc):
        units = []
        for bl in range(group):
            start = pl.multiple_of(base + bl * t, t)
            if bl == 0:
                before = pl.multiple_of(jnp.maximum(base - t, 0), t)
                first = base == 0
                kp = jnp.where(first, kp0, kc[pl.ds(before, t), :])
                vp = jnp.where(first, vp0, vc[pl.ds(before, t), :])
                ok = jnp.logical_or(has_prev, base > 0)
            else:
                before = pl.multiple_of(base + (bl - 1) * t, t)
                kp, vp, ok = kc[pl.ds(before, t), :], vc[pl.ds(before, t), :], True
            units.append((q[pl.ds(start, t), :], kp, kc[pl.ds(start, t), :], vp, vc[pl.ds(start, t), :], ok))
        return units

    groups4 = blocks4 // group

    def body4(i, carry):
        r = i // groups4
        base = pl.multiple_of((i % groups4) * (group * t), group * t)

        def make():
            return stream_units(base, q4.at[0, r], k4p[0, r], k4c.at[0, r], v4p[0, r], v4c.at[0, r])

        def consume(outs):
            for bl, res in enumerate(outs):
                rows = pl.ds((base + bl * t) * d4 + r, t, stride=d4)
                store_state(rows, _merge(load_state(rows), res[0]))

        _attn_pipeline([(make, None, consume)], head_ones=head_ones)
        return carry

    lax.fori_loop(0, d4 * groups4, body4, 0)

    def body1(g, carry):
        base = pl.multiple_of(g * (group * t), group * t)

        def make():
            return stream_units(base, q1.at[0], k1p[0], k1c.at[0], v1p[0], v1c.at[0])

        def consume(outs):
            for bl, res in enumerate(outs):
                rows = pl.ds(pl.multiple_of(base + bl * t, t), t)
                acc, _, l = _merge(load_state(rows), res[0])
                o_ref[0, rows, :] = (acc * (1.0 / l)).astype(o_ref.dtype)

        _attn_pipeline([(make, None, consume)], head_ones=head_ones)
        return carry

    lax.fori_loop(0, blocks1 // group, body1, 0)


def _attn_a(a1, a4, a16, head_ones):
    b, s, _ = a1.shape
    t = ATT_BLOCK
    (w1, d1), (w4, d4), (w16, d16) = DILATED_PAIRS
    assert d1 == 1 and w1 == t and w4 == t * d4 and w16 == t * d16 and d16 % d4 == 0
    sup = t * d16
    assert s % sup == 0
    n_chunks = A_WIDTH // HEAD_CHUNK
    qc, kc, vc = 0, n_chunks, 2 * n_chunks

    def nat(rows, per_sup, off, prev):
        def imap(bb, n, c):
            blk = jnp.maximum(n * per_sup - 1, 0) if prev else n
            return (bb, blk, off + c)
        return pl.BlockSpec((1, rows, HEAD_CHUNK), imap)

    def strm(d, rows, per_sup, off, prev):
        def imap(bb, n, c):
            blk = jnp.maximum(n * per_sup - 1, 0) if prev else n
            return (bb, 0, blk, off + c)
        return pl.BlockSpec((1, d, rows, HEAD_CHUNK), imap)

    in_specs = [_resident((HEAD_CHUNK, HEAD_CHUNK), lambda bb, n, c: (0, 0)),
                nat(sup, 1, qc, False),
                nat(t, sup // t, kc, True), nat(sup, 1, kc, False),
                nat(t, sup // t, vc, True), nat(sup, 1, vc, False),
                strm(d4, sup // d4, 1, qc, False),
                strm(d4, t, sup // d4 // t, kc, True), strm(d4, sup // d4, 1, kc, False),
                strm(d4, t, sup // d4 // t, vc, True), strm(d4, sup // d4, 1, vc, False),
                strm(d16, t, 1, qc, False),
                strm(d16, t, 1, kc, True), strm(d16, t, 1, kc, False),
                strm(d16, t, 1, vc, True), strm(d16, t, 1, vc, False)]
    state = pltpu.VMEM((N_SLABS, sup, LANES), _f32)
    return pl.pallas_call(
        functools.partial(_attn_a_kernel, d4=d4, d16=d16),
        out_shape=jax.ShapeDtypeStruct((b, s, A_WIDTH), _bf16),
        grid=(b, s // sup, n_chunks),
        in_specs=in_specs,
        out_specs=pl.BlockSpec((1, sup, HEAD_CHUNK), lambda bb, n, c: (bb, n, c)),
        scratch_shapes=[state, state, state],
        compiler_params=pltpu.CompilerParams(
            dimension_semantics=("parallel", "parallel", "parallel"), vmem_limit_bytes=V7X_VMEM_LIMIT),
        name="attn_a",
    )(head_ones, a1, a1, a1, a1, a1, a4, a4, a4, a4, a4, a16, a16, a16, a16, a16).reshape(b * s, A_WIDTH)


def _attn_c_kernel(sink_ref, q_ref, kp_ref, kc_ref, vp_ref, vc_ref, o_ref, *, blocks, group):
    t = ATT_BLOCK
    g = pl.program_id(1)
    has_prev = pl.program_id(2) > 0
    n = C_WIDTH // C_KV_HEADS // HEAD_CHUNK
    sink_rows = [sink_ref[g * n + c] for c in range(n)]
    groups = []
    for b0 in range(0, blocks, group):
        def make(b0=b0):
            units = []
            for blk in range(b0, b0 + group):
                cur = slice(blk * t, (blk + 1) * t)
                if blk == 0:
                    kp, vp, ok = kp_ref[0], vp_ref[0], has_prev
                else:
                    before = slice((blk - 1) * t, blk * t)
                    kp, vp, ok = kc_ref[0, before], vc_ref[0, before], True
                units.append((q_ref[0, cur], kp, kc_ref[0, cur], vp, vc_ref[0, cur], ok))
            return units

        def consume(outs, b0=b0):
            for blk, res in zip(range(b0, b0 + group), outs):
                for c, (acc, _, l) in enumerate(res):
                    o_ref[0, blk * t:(blk + 1) * t, c * HEAD_CHUNK:(c + 1) * HEAD_CHUNK] = (
                        acc * (1.0 / l)).astype(o_ref.dtype)

        groups.append((make, [sink_rows] * group, consume))
    _attn_pipeline(groups)


def _attn_c(rest, kv_rep, sink_lanes, *, blocks, group):
    b, s, _ = rest.shape
    t = ATT_BLOCK
    rows = blocks * t
    group_width = C_WIDTH // C_KV_HEADS
    q_off = 0
    assert C_WINDOW == t and s % rows == 0 and q_off % group_width == 0 and blocks % group == 0
    q_blk = q_off // group_width

    def kv(off, prev):
        if prev:
            return pl.BlockSpec((1, t, HEAD_CHUNK), lambda bb, g, i: (bb, jnp.maximum(i * blocks - 1, 0), off + g))
        return pl.BlockSpec((1, rows, HEAD_CHUNK), lambda bb, g, i: (bb, i, off + g))

    return pl.pallas_call(
        functools.partial(_attn_c_kernel, blocks=blocks, group=group),
        out_shape=jax.ShapeDtypeStruct((b, s, C_WIDTH), _bf16),
        grid=(b, C_KV_HEADS, s // rows),
        in_specs=[
            _resident(sink_lanes.shape, lambda bb, g, i: (0, 0, 0)),
            pl.BlockSpec((1, rows, group_width), lambda bb, g, i: (bb, i, q_blk + g)),
            kv(0, True), kv(0, False), kv(C_KV_HEADS, True), kv(C_KV_HEADS, False),
        ],
        out_specs=pl.BlockSpec((1, rows, group_width), lambda bb, g, i: (bb, i, g)),
        compiler_params=pltpu.CompilerParams(
            dimension_semantics=("parallel", "parallel", "parallel"), vmem_limit_bytes=V7X_VMEM_LIMIT),
        name="attn_c",
    )(sink_lanes, rest, kv_rep, kv_rep, kv_rep, kv_rep).reshape(b * s, C_WIDTH)


def _out_proj_kernel(oa_ref, ob_ref, oc_ref, x_ref, w_ref, xo_ref):
    acc = jnp.dot(oa_ref[...], w_ref[0, 0:A_WIDTH, :], preferred_element_type=_f32)
    acc = acc + jnp.dot(ob_ref[...], w_ref[0, A_WIDTH:A_WIDTH + B_WIDTH, :], preferred_element_type=_f32)
    acc = acc + jnp.dot(oc_ref[...], w_ref[0, A_WIDTH + B_WIDTH:, :], preferred_element_type=_f32)
    xo_ref[...] = x_ref[...] + acc


def _out_proj(oa, ob, oc, x, w, layer, *, tm):
    m, d = x.shape
    assert m % tm == 0

    def rows(width):
        return pl.BlockSpec((tm, width), lambda i: (i, 0))

    return pl.pallas_call(
        _out_proj_kernel,
        out_shape=jax.ShapeDtypeStruct((m, d), _f32),
        grid=(m // tm,),
        in_specs=[rows(A_WIDTH), rows(B_WIDTH), rows(C_WIDTH), rows(d),
                  _resident((1,) + w.shape[1:], lambda i: (layer, 0, 0))],
        out_specs=rows(d),
        compiler_params=pltpu.CompilerParams(
            dimension_semantics=("parallel",), vmem_limit_bytes=V7X_VMEM_LIMIT),
        name="out_proj",
    )(oa, ob, oc, x, w)


def _ffn_kernel(x_ref, g_ref, wg_ref, wu_ref, wd_ref, o_ref, h_ref):
    f = pl.program_id(1)

    @pl.when(f == 0)
    def _():
        x = x_ref[...]
        ms = jnp.mean(x * x, axis=-1, keepdims=True)
        h_ref[...] = (x * lax.rsqrt(ms + EPS) * g_ref[...]).astype(_bf16)
        o_ref[...] = x

    h = h_ref[...]
    gate = jnp.dot(h, wg_ref[0], preferred_element_type=_f32)
    up = jnp.dot(h, wu_ref[0], preferred_element_type=_f32)
    act = (gate * _sigmoid(gate) * up).astype(_bf16)
    o_ref[...] += jnp.dot(act, wd_ref[0], preferred_element_type=_f32)


def _ffn(x, g, wg, wu, wd, layer, *, tm, tf):
    m, d = x.shape
    ff = wg.shape[2]
    assert m % tm == 0 and ff % tf == 0
    return pl.pallas_call(
        _ffn_kernel,
        out_shape=jax.ShapeDtypeStruct((m, d), _f32),
        grid=(m // tm, ff // tf),
        in_specs=[
            pl.BlockSpec((tm, d), lambda i, f: (i, 0)),
            pl.BlockSpec((1, d), lambda i, f: (0, 0)),
            pl.BlockSpec((1, d, tf), lambda i, f: (layer, 0, f)),
            pl.BlockSpec((1, d, tf), lambda i, f: (layer, 0, f)),
            pl.BlockSpec((1, tf, d), lambda i, f: (layer, f, 0)),
        ],
        out_specs=pl.BlockSpec((tm, d), lambda i, f: (i, 0)),
        scratch_shapes=[pltpu.VMEM((tm, d), _bf16)],
        compiler_params=pltpu.CompilerParams(
            dimension_semantics=("parallel", "arbitrary"), vmem_limit_bytes=V7X_VMEM_LIMIT),
        name="swiglu_ffn",
    )(x, g, wg, wu, wd)


def _tile(m, pref):
    t = pref
    while m % t:
        t //= 2
    return t


def _qk_gain_and_flag(a_q_g, a_k_g, c_q_g, c_k_g):
    scale = HEAD_DIM ** -0.5 * LOG2E
    ones = lambda n: jnp.ones((n,), _f32)
    zeros = lambda n: jnp.zeros((n,), _f32)
    gain = jnp.concatenate([
        jnp.tile(a_q_g * scale, A_HEADS), jnp.tile(a_k_g, A_HEADS), ones(A_WIDTH + 2 * B_WIDTH),
        jnp.tile(c_q_g * scale, C_Q_HEADS), jnp.tile(c_k_g, C_KV_HEADS), ones(C_KV_WIDTH)])
    flag = jnp.concatenate([ones(2 * A_WIDTH), zeros(A_WIDTH + 2 * B_WIDTH),
                            ones(C_WIDTH + C_KV_WIDTH), zeros(C_KV_WIDTH)])
    return gain.reshape(1, IN_WIDTH), flag.reshape(1, IN_WIDTH)


def kernel(x, norm1_g, w_in, a_q_g, a_k_g, conv_w, conv_b, conv_ln_g, conv_ln_b,
           c_q_g, c_k_g, c_sinks, w_out, norm2_g, w_gate, w_up, w_down):
    b, s, d = x.shape
    m = b * s
    depth = w_in.shape[0]
    lane = jnp.arange(HEAD_CHUNK) // HEAD_DIM
    head_ones = (lane[:, None] == lane[None, :]).astype(_bf16)
    dst = jnp.arange(KV_REP_WIDTH)
    kv_rep = (jnp.arange(HEAD_CHUNK)[:, None] ==
              (dst // HEAD_CHUNK) * HEAD_DIM + dst % HEAD_DIM).astype(_bf16)
    w_in, w_out, w_gate, w_up, w_down = (w.astype(_bf16) for w in (w_in, w_out, w_gate, w_up, w_down))
    xf = x.reshape(m, d)
    for l in range(depth):
        gain, flag = _qk_gain_and_flag(a_q_g[l], a_k_g[l], c_q_g[l], c_k_g[l])
        a1, a4, a16, ob, rest, kvr = _in_proj(
            xf, norm1_g[l].reshape(1, d), w_in, l, head_ones, kv_rep, gain, flag,
            conv_w[l], conv_b[l].reshape(1, -1), conv_ln_g[l].reshape(1, -1), conv_ln_b[l].reshape(1, -1),
            batch=b, tm=_tile(s, 512))
        oa = _attn_a(a1.reshape(b, s, A_QKV_WIDTH), a4, a16, head_ones)
        rest = rest.reshape(b, s, REST_WIDTH)
        sink_lanes = jnp.repeat(c_sinks[l] * LOG2E, HEAD_DIM).reshape(C_WIDTH // HEAD_CHUNK, 1, HEAD_CHUNK)
        oc = _attn_c(rest, kvr.reshape(b, s, KV_REP_WIDTH), sink_lanes, blocks=_tile(s // ATT_BLOCK, 4), group=2)
        xf = _out_proj(oa, ob, oc, xf, w_out, l, tm=_tile(m, 512))
        xf = _ffn(xf, norm2_g[l].reshape(1, d), w_gate, w_up, w_down, l, tm=_tile(m, 1024), tf=512)
    return xf.reshape(b, s, d)
```

```python
import functools
import math

import jax
import jax.numpy as jnp
from jax import lax
from jax.experimental import pallas as pl
from jax.experimental.pallas import tpu as pltpu

HEAD_DIM = 64
A_HEADS = 8
A_WIDTH = A_HEADS * HEAD_DIM
DILATED_PAIRS = ((128, 1), (512, 4), (2048, 16))
B_WIDTH = 512
CONV_WIDTH = 31
C_Q_HEADS = 16
C_KV_HEADS = 2
C_WIDTH = C_Q_HEADS * HEAD_DIM
C_KV_WIDTH = C_KV_HEADS * HEAD_DIM
C_WINDOW = 128
IN_WIDTH = 3 * A_WIDTH + 2 * B_WIDTH + C_WIDTH + 2 * C_KV_WIDTH
EPS = 1e-6
LOG2E = math.log2(math.e)

LANES = 128
SUBLANES = 8
BF16_ROWS = 16
ATT_BLOCK = 128
HEAD_CHUNK = 256
HEADS_PER_CHUNK = HEAD_CHUNK // HEAD_DIM
N_SLABS = HEAD_CHUNK // LANES
A_QKV_WIDTH = 3 * A_WIDTH
REST_WIDTH = C_WIDTH + 2 * C_KV_WIDTH
KV_REP_WIDTH = 2 * C_KV_HEADS * HEAD_CHUNK
CONV_HALO = 32
MASKED_SCORE = -1e30
V7X_VMEM_LIMIT = 56 * 1024 * 1024

_bf16 = jnp.bfloat16
_f32 = jnp.float32


def _sigmoid(v):
    return 1.0 / (1.0 + jnp.exp(-v))


def _resident(shape, index_map):
    return pl.BlockSpec(shape, index_map, pipeline_mode=pl.Buffered(1))


CONV_ROWS = 64


def _conv_tasks(h_halves, is_first, w_ref, b_ref, lg_ref, lb_ref, ob_ref, hext_ref, shift_ref, y_ref, *, tm):
    col_groups = [slice(cg * LANES, (cg + 1) * LANES) for cg in range(B_WIDTH // LANES)]
    shift_rows = tm + CONV_HALO - SUBLANES
    first = CONV_HALO - (CONV_WIDTH - 1)

    def fill(pace):
        tail = hext_ref[tm:tm + CONV_HALO, :]
        hext_ref[0:CONV_HALO, :] = jnp.where(is_first, 0.0, tail)
        for half, h in enumerate(h_halves):
            hext_ref[CONV_HALO:CONV_HALO + tm, half * HEAD_CHUNK:(half + 1) * HEAD_CHUNK] = h

    def shift(j, pace):
        for cs in col_groups:
            shift_ref[j - 1, :, cs] = hext_ref[j:j + shift_rows, cs]

    def taps(cs, r0, pace):
        acc = jnp.zeros((CONV_ROWS, LANES), _f32) + pace
        for k in range(CONV_WIDTH):
            a, j = divmod(first + k, SUBLANES)
            rows = slice(r0 + a * SUBLANES, r0 + a * SUBLANES + CONV_ROWS)
            tap = hext_ref[rows, cs] if j == 0 else shift_ref[j - 1, rows, cs]
            acc = acc + w_ref[k:k + 1, cs] * tap
        y_ref[r0:r0 + CONV_ROWS, cs] = acc + b_ref[:, cs]

    def finish(r0, pace):
        y = y_ref[r0:r0 + CONV_ROWS, :]
        mu = jnp.mean(y, axis=-1, keepdims=True)
        yc = y - mu
        var = jnp.mean(yc * yc, axis=-1, keepdims=True)
        z = yc * lax.rsqrt(var + EPS) * lg_ref[...] + lb_ref[...]
        ob_ref[r0:r0 + CONV_ROWS, :] = (z * _sigmoid(z)).astype(ob_ref.dtype)

    tasks = [fill] + [functools.partial(shift, j) for j in range(1, SUBLANES)]
    for r0 in range(0, tm, CONV_ROWS):
        tasks += [functools.partial(taps, cs, r0) for cs in col_groups]
        tasks.append(functools.partial(finish, r0))
    return tasks


def _in_proj_kernel(x_ref, g_ref, w_ref, e_ref, rep_ref, gain_ref, flag_ref, cw_ref, cb_ref, lg_ref, lb_ref,
                    a1_ref, a4_ref, a16_ref, ob_ref, rest_ref, kv_ref,
                    h_ref, slab_ref, hext_ref, shift_ref, y_ref, *, tm, tiles_per_seq):
    x = x_ref[...]
    ms = jnp.mean(x * x, axis=-1, keepdims=True)
    h_ref[...] = (x * lax.rsqrt(ms + EPS) * g_ref[...]).astype(_bf16)

    n_chunks = IN_WIDTH // HEAD_CHUNK
    a_chunks = A_QKV_WIDTH // HEAD_CHUNK
    qk_chunks = 2 * A_WIDTH // HEAD_CHUNK
    conv_chunks = 2 * B_WIDTH // HEAD_CHUNK
    cq_chunk = a_chunks + conv_chunks
    order = list(range(a_chunks, cq_chunk)) + list(range(a_chunks)) + list(range(cq_chunk, n_chunks))

    def project(chunk):
        return jnp.dot(h_ref[...], w_ref[0, :, chunk * HEAD_CHUNK:(chunk + 1) * HEAD_CHUNK],
                       preferred_element_type=_f32)

    conv_in, conv_todo = [], []
    a_next = project(order[0])
    for pos, chunk in enumerate(order):
        sl = slice(chunk * HEAD_CHUNK, (chunk + 1) * HEAD_CHUNK)
        a = a_next
        if pos + 1 < n_chunks:
            a_next = project(order[pos + 1])
        share = -(-len(conv_todo) // max(n_chunks - 1 - pos, 1))
        if share:
            bits = pltpu.bitcast(a[0:SUBLANES, 0:LANES], jnp.int32)
            pace = lax.shift_right_logical(lax.shift_right_logical(bits, 16), 16)[0:1, :].astype(_f32)
            for task in conv_todo[:share]:
                task(pace)
            del conv_todo[:share]
        if chunk < qk_chunks or chunk >= cq_chunk:
            ss = jnp.dot((a * a).astype(_bf16), e_ref[...], preferred_element_type=_f32)
            yn = a * lax.rsqrt(ss * (1.0 / HEAD_DIM) + EPS) * gain_ref[:, sl]
            a = jnp.where(flag_ref[:, sl] > 0.0, yn, a) if chunk == n_chunks - 1 else yn
        if chunk < a_chunks:
            a1_ref[:, sl] = a.astype(_bf16)
            base = (chunk % 2) * N_SLABS
            for s in range(N_SLABS):
                slab_ref[base + s] = a[:, s * LANES:(s + 1) * LANES]
            for s in range(N_SLABS):
                cs = slice(chunk * HEAD_CHUNK + s * LANES, chunk * HEAD_CHUNK + (s + 1) * LANES)
                for (_, d), ref in zip(DILATED_PAIRS[1:], (a4_ref, a16_ref)):
                    for r in range(d):
                        ref[0, r, :, cs] = slab_ref[base + s, pl.ds(r, tm // d, stride=d), :].astype(_bf16)
        elif chunk < cq_chunk:
            conv_in.append(a)
            if len(conv_in) == conv_chunks:
                half = conv_chunks // 2
                glu = [conv_in[c] * _sigmoid(conv_in[half + c]) for c in range(half)]
                conv_todo = _conv_tasks(glu, pl.program_id(0) % tiles_per_seq == 0, cw_ref, cb_ref, lg_ref,
                                        lb_ref, ob_ref, hext_ref, shift_ref, y_ref, tm=tm)
        else:
            rsl = slice(sl.start - cq_chunk * HEAD_CHUNK, sl.stop - cq_chunk * HEAD_CHUNK)
            ab = a.astype(_bf16)
            rest_ref[:, rsl] = ab
            if chunk == n_chunks - 1:
                kv_ref[...] = jnp.dot(ab, rep_ref[...], preferred_element_type=_f32).astype(_bf16)


def _in_proj(x, g, w, layer, head_ones, kv_rep, gain, flag, conv_w, conv_b, ln_g, ln_b, *, batch, tm):
    m, d = x.shape
    s = m // batch
    n = w.shape[2]
    assert n == IN_WIDTH and s % tm == 0 and (IN_WIDTH - 2 * C_KV_WIDTH) % HEAD_CHUNK == 0
    assert tm % CONV_ROWS == 0 and CONV_HALO >= CONV_WIDTH - 1 and CONV_HALO % SUBLANES == 0
    tiles_per_seq = s // tm

    def stream_spec(dil):
        return pl.BlockSpec((1, dil, tm // dil, A_QKV_WIDTH),
                            lambda i: (i // tiles_per_seq, 0, i % tiles_per_seq, 0))

    def rows(width):
        return pl.BlockSpec((tm, width), lambda i: (i, 0))

    def whole(arr):
        return _resident(arr.shape, lambda i: (0,) * arr.ndim)

    d4, d16 = DILATED_PAIRS[1][1], DILATED_PAIRS[2][1]
    return pl.pallas_call(
        functools.partial(_in_proj_kernel, tm=tm, tiles_per_seq=tiles_per_seq),
        out_shape=(jax.ShapeDtypeStruct((m, A_QKV_WIDTH), _bf16),
                   jax.ShapeDtypeStruct((batch, d4, s // d4, A_QKV_WIDTH), _bf16),
                   jax.ShapeDtypeStruct((batch, d16, s // d16, A_QKV_WIDTH), _bf16),
                   jax.ShapeDtypeStruct((m, B_WIDTH), _bf16),
                   jax.ShapeDtypeStruct((m, REST_WIDTH), _bf16),
                   jax.ShapeDtypeStruct((m, KV_REP_WIDTH), _bf16)),
        grid=(m // tm,),
        in_specs=[
            rows(d),
            whole(g),
            _resident((1, d, n), lambda i: (layer, 0, 0)),
            whole(head_ones), whole(kv_rep), whole(gain), whole(flag),
            whole(conv_w), whole(conv_b), whole(ln_g), whole(ln_b),
        ],
        out_specs=(rows(A_QKV_WIDTH), stream_spec(d4), stream_spec(d16), rows(B_WIDTH), rows(REST_WIDTH),
                   rows(KV_REP_WIDTH)),
        scratch_shapes=[pltpu.VMEM((tm, d), _bf16), pltpu.VMEM((2 * N_SLABS, tm, LANES), _f32),
                        pltpu.VMEM((CONV_HALO + tm, B_WIDTH), _f32),
                        pltpu.VMEM((SUBLANES - 1, tm + CONV_HALO - SUBLANES, B_WIDTH), _f32),
                        pltpu.VMEM((tm, B_WIDTH), _f32)],
        compiler_params=pltpu.CompilerParams(
            dimension_semantics=("arbitrary",), vmem_limit_bytes=V7X_VMEM_LIMIT),
        name="in_proj",
    )(x, g, w, head_ones, kv_rep, gain, flag, conv_w, conv_b, ln_g, ln_b)


def _per_head_lanes(cols):
    half = lax.broadcasted_iota(jnp.int32, (1, LANES), 1) < HEAD_DIM
    return jnp.concatenate([jnp.where(half, cols[0], cols[1]), jnp.where(half, cols[2], cols[3])], axis=1)


def _attn_consts():
    t = ATT_BLOCK
    lane_head = lax.broadcasted_iota(jnp.int32, (1, HEAD_CHUNK), 1) // HEAD_DIM
    head_mask = [(lane_head == h).astype(_bf16) for h in range(HEADS_PER_CHUNK)]
    lower = lax.broadcasted_iota(jnp.int32, (t, t), 1) <= lax.broadcasted_iota(jnp.int32, (t, t), 0)
    return head_mask, lower, lower.astype(_bf16)


def _attn_scores(units, consts, head_ones):
    head_mask = consts[0]
    scores, far_scores = [], []
    for (q, kp, kc, vp, vc, has_prev) in units:
        n = q.shape[1] // HEAD_CHUNK
        qs = jnp.concatenate([q[:, c * HEAD_CHUNK:(c + 1) * HEAD_CHUNK] * head_mask[h]
                              for c in range(n) for h in range(HEADS_PER_CHUNK)], axis=0)
        kcat = jnp.concatenate([kp, kc], axis=0)
        scores.append(lax.dot_general(qs, kcat, (((1,), (1,)), ((), ())), preferred_element_type=_f32))
    if head_ones is not None:
        for (q, kp, kc, vp, vc, has_prev) in units:
            far_scores.append(jnp.dot(q * kp, head_ones, preferred_element_type=_f32))
    return scores, far_scores


def _attn_finish(units, staged, consts, head_ones, sinks):
    t = ATT_BLOCK
    head_mask, lower, lower_bf = consts
    scores, far_scores = staged
    far_key = head_ones is not None
    stage = []
    for u, (q, kp, kc, vp, vc, has_prev) in enumerate(units):
        per_chunk = []
        for c in range(q.shape[1] // HEAD_CHUNK):
            ms, ls, pcs = [], [], []
            for h in range(HEADS_PER_CHUNK):
                base = (c * HEADS_PER_CHUNK + h) * t
                s2 = scores[u][base:base + t]
                s_prev = s2[:, :t]
                if has_prev is not True:
                    s_prev = jnp.where(has_prev, s_prev, MASKED_SCORE)
                s = jnp.where(lower, s2[:, t:], s_prev)
                m = jnp.max(s, axis=-1, keepdims=True)
                p = jnp.exp2(s - m)
                ls.append(jnp.sum(p, axis=-1, keepdims=True))
                ms.append(m)
                pb = p.astype(_bf16)
                p_cur = pb * lower_bf
                pcs.append(pb - p_cur)
                pcs.append(p_cur)
            m_all = _per_head_lanes(ms)
            l_all = _per_head_lanes(ls)
            rescale = extra_w = None
            if far_key or sinks is not None:
                if far_key:
                    es = far_scores[u]
                    if has_prev is not True:
                        es = jnp.where(has_prev, es, MASKED_SCORE)
                else:
                    es = sinks[u][c]
                m_new = jnp.maximum(m_all, es)
                rescale = jnp.exp2(m_all - m_new)
                extra_w = jnp.exp2(es - m_new)
                l_all = l_all * rescale + extra_w
                m_all = m_new
            per_chunk.append((jnp.concatenate(pcs, axis=1), m_all, l_all, rescale, extra_w))
        stage.append(per_chunk)

    outs = []
    for u, (q, kp, kc, vp, vc, has_prev) in enumerate(units):
        vcat = jnp.concatenate([vp, vc], axis=0)
        v_bd = jnp.concatenate([vcat * head_mask[h] for h in range(HEADS_PER_CHUNK)], axis=0)
        res = []
        for (p_all, m_all, l_all, rescale, extra_w) in stage[u]:
            acc = jnp.dot(p_all, v_bd, preferred_element_type=_f32)
            if rescale is not None:
                acc = acc * rescale
            if far_key:
                acc = acc + extra_w * vp.astype(_f32)
            res.append((acc, m_all, l_all))
        outs.append(res)
    return outs


def _attn_pipeline(groups, *, head_ones=None):
    consts = _attn_consts()
    pending = None
    for grp in list(groups) + [None]:
        nxt = None
        if grp is not None:
            units = grp[0]()
            nxt = (grp, units, _attn_scores(units, consts, head_ones))
        if pending is not None:
            (_, sinks, consume), units_p, staged = pending
            consume(_attn_finish(units_p, staged, consts, head_ones, sinks))
        pending = nxt


def _merge(old, new):
    acc_o, m_o, l_o = old
    acc_n, m_n, l_n = new
    m = jnp.maximum(m_o, m_n)
    w_o = jnp.exp2(m_o - m)
    w_n = jnp.exp2(m_n - m)
    return w_o * acc_o + w_n * acc_n, m, w_o * l_o + w_n * l_n


def _attn_a_kernel(e_ref, q1, k1p, k1c, v1p, v1c, q4, k4p, k4c, v4p, v4c, q16, k16p, k16c, v16p, v16c,
                   o_ref, acc_ref, m_ref, l_ref, *, d4, d16):
    t = ATT_BLOCK
    has_prev = pl.program_id(1) > 0
    group16, group = 2, 4
    blocks4 = d16 // d4
    blocks1 = d16
    state = (acc_ref, m_ref, l_ref)

    def slab(x, s):
        return x[:, s * LANES:(s + 1) * LANES]

    def load_state(rows):
        return tuple(jnp.concatenate([ref[s, rows, :] for s in range(N_SLABS)], axis=1) for ref in state)

    def store_state(rows, vals):
        for ref, val in zip(state, vals):
            for s in range(N_SLABS):
                ref[s, rows, :] = slab(val, s)

    head_ones = e_ref[...]

    def body16(rg, carry):
        rs = [group16 * rg + u for u in range(group16)]

        def make():
            return [(q16[0, r], k16p[0, r], k16c[0, r], v16p[0, r], v16c[0, r], has_prev) for r in rs]

        def consume(outs):
            for r, res in zip(rs, outs):
                store_state(pl.ds(r, t, stride=d16), res[0])

        _attn_pipeline([(make, None, consume)], head_ones=head_ones)
        return carry

    lax.fori_loop(0, d16 // group16, body16, 0)

    def stream_units(base, q, kp0, kc, vp0, vc):
        units = []
        for bl in range(group):
            start = pl.multiple_of(base + bl * t, t)
            if bl == 0:
                before = pl.multiple_of(jnp.maximum(base - t, 0), t)
                first = base == 0
                kp = jnp.where(first, kp0, kc[pl.ds(before, t), :])
                vp = jnp.where(first, vp0, vc[pl.ds(before, t), :])
                ok = jnp.logical_or(has_prev, base > 0)
            else:
                before = pl.multiple_of(base + (bl - 1) * t, t)
                kp, vp, ok = kc[pl.ds(before, t), :], vc[pl.ds(before, t), :], True
            units.append((q[pl.ds(start, t), :], kp, kc[pl.ds(start, t), :], vp, vc[pl.ds(start, t), :], ok))
        return units

    groups4 = blocks4 // group

    def body4(i, carry):
        r = i // groups4
        base = pl.multiple_of((i % groups4) * (group * t), group * t)

        def make():
            return stream_units(base, q4.at[0, r], k4p[0, r], k4c.at[0, r], v4p[0, r], v4c.at[0, r])

        def consume(outs):
            for bl, res in enumerate(outs):
                rows = pl.ds((base + bl * t) * d4 + r, t, stride=d4)
                store_state(rows, _merge(load_state(rows), res[0]))

        _attn_pipeline([(make, None, consume)], head_ones=head_ones)
        return carry

    lax.fori_loop(0, d4 * groups4, body4, 0)

    def body1(g, carry):
        base = pl.multiple_of(g * (group * t), group * t)

        def make():
            return stream_units(base, q1.at[0], k1p[0], k1c.at[0], v1p[0], v1c.at[0])

        def consume(outs):
            for bl, res in enumerate(outs):
                rows = pl.ds(pl.multiple_of(base + bl * t, t), t)
                acc, _, l = _merge(load_state(rows), res[0])
                o_ref[0, rows, :] = (acc * (1.0 / l)).astype(o_ref.dtype)

        _attn_pipeline([(make, None, consume)], head_ones=head_ones)
        return carry

    lax.fori_loop(0, blocks1 // group, body1, 0)


def _attn_a(a1, a4, a16, head_ones):
    b, s, _ = a1.shape
    t = ATT_BLOCK
    (w1, d1), (w4, d4), (w16, d16) = DILATED_PAIRS
    assert d1 == 1 and w1 == t and w4 == t * d4 and w16 == t * d16 and d16 % d4 == 0
    sup = t * d16
    assert s % sup == 0
    n_chunks = A_WIDTH // HEAD_CHUNK
    qc, kc, vc = 0, n_chunks, 2 * n_chunks

    def nat(rows, per_sup, off, prev):
        def imap(bb, n, c):
            blk = jnp.maximum(n * per_sup - 1, 0) if prev else n
            return (bb, blk, off + c)
        return pl.BlockSpec((1, rows, HEAD_CHUNK), imap)

    def strm(d, rows, per_sup, off, prev):
        def imap(bb, n, c):
            blk = jnp.maximum(n * per_sup - 1, 0) if prev else n
            return (bb, 0, blk, off + c)
        return pl.BlockSpec((1, d, rows, HEAD_CHUNK), imap)

    in_specs = [_resident((HEAD_CHUNK, HEAD_CHUNK), lambda bb, n, c: (0, 0)),
                nat(sup, 1, qc, False),
                nat(t, sup // t, kc, True), nat(sup, 1, kc, False),
                nat(t, sup // t, vc, True), nat(sup, 1, vc, False),
                strm(d4, sup // d4, 1, qc, False),
                strm(d4, t, sup // d4 // t, kc, True), strm(d4, sup // d4, 1, kc, False),
                strm(d4, t, sup // d4 // t, vc, True), strm(d4, sup // d4, 1, vc, False),
                strm(d16, t, 1, qc, False),
                strm(d16, t, 1, kc, True), strm(d16, t, 1, kc, False),
                strm(d16, t, 1, vc, True), strm(d16, t, 1, vc, False)]
    state = pltpu.VMEM((N_SLABS, sup, LANES), _f32)
    return pl.pallas_call(
        functools.partial(_attn_a_kernel, d4=d4, d16=d16),
        out_shape=jax.ShapeDtypeStruct((b, s, A_WIDTH), _bf16),
        grid=(b, s // sup, n_chunks),
        in_specs=in_specs,
        out_specs=pl.BlockSpec((1, sup, HEAD_CHUNK), lambda bb, n, c: (bb, n, c)),
        scratch_shapes=[state, state, state],
        compiler_params=pltpu.CompilerParams(
            dimension_semantics=("parallel", "parallel", "parallel"), vmem_limit_bytes=V7X_VMEM_LIMIT),
        name="attn_a",
    )(head_ones, a1, a1, a1, a1, a1, a4, a4, a4, a4, a4, a16, a16, a16, a16, a16).reshape(b * s, A_WIDTH)


def _attn_c_kernel(sink_ref, q_ref, kp_ref, kc_ref, vp_ref, vc_ref, wg_ref, wu_ref, wd_ref,
                   o_ref, wg_out, wu_out, wd_out, *, blocks, group, cast_every):
    t = ATT_BLOCK
    g = pl.program_id(1)
    has_prev = pl.program_id(2) > 0

    step = (pl.program_id(0) * pl.num_programs(1) + g) * pl.num_programs(2) + pl.program_id(2)

    @pl.when(step % cast_every == 0)
    def _():
        for src, dst in ((wg_ref, wg_out), (wu_ref, wu_out), (wd_ref, wd_out)):
            dst[...] = src[0].astype(dst.dtype)

    n = C_WIDTH // C_KV_HEADS // HEAD_CHUNK
    sink_rows = [sink_ref[g * n + c] for c in range(n)]
    groups = []
    for b0 in range(0, blocks, group):
        def make(b0=b0):
            units = []
            for blk in range(b0, b0 + group):
                cur = slice(blk * t, (blk + 1) * t)
                if blk == 0:
                    kp, vp, ok = kp_ref[0], vp_ref[0], has_prev
                else:
                    before = slice((blk - 1) * t, blk * t)
                    kp, vp, ok = kc_ref[0, before], vc_ref[0, before], True
                units.append((q_ref[0, cur], kp, kc_ref[0, cur], vp, vc_ref[0, cur], ok))
            return units

        def consume(outs, b0=b0):
            for blk, res in zip(range(b0, b0 + group), outs):
                for c, (acc, _, l) in enumerate(res):
                    o_ref[0, blk * t:(blk + 1) * t, c * HEAD_CHUNK:(c + 1) * HEAD_CHUNK] = (
                        acc * (1.0 / l)).astype(o_ref.dtype)

        groups.append((make, [sink_rows] * group, consume))
    _attn_pipeline(groups)


def _attn_c(rest, kv_rep, sink_lanes, ffn_weights, layer, *, blocks, group):
    b, s, _ = rest.shape
    t = ATT_BLOCK
    rows = blocks * t
    steps = b * C_KV_HEADS * (s // rows)
    pieces = math.gcd(steps, *(w.shape[1] // BF16_ROWS for w in ffn_weights))
    cast_every = steps // pieces

    def w_in_spec(w):
        return pl.BlockSpec((1, w.shape[1] // pieces, w.shape[2]),
                            lambda bb, g, i: (layer, ((bb * C_KV_HEADS + g) * (s // rows) + i) // cast_every, 0))

    def w_out_spec(w):
        return pl.BlockSpec((w.shape[1] // pieces, w.shape[2]),
                            lambda bb, g, i: (((bb * C_KV_HEADS + g) * (s // rows) + i) // cast_every, 0))
    group_width = C_WIDTH // C_KV_HEADS
    q_off = 0
    assert C_WINDOW == t and s % rows == 0 and q_off % group_width == 0 and blocks % group == 0
    q_blk = q_off // group_width

    def kv(off, prev):
        if prev:
            return pl.BlockSpec((1, t, HEAD_CHUNK), lambda bb, g, i: (bb, jnp.maximum(i * blocks - 1, 0), off + g))
        return pl.BlockSpec((1, rows, HEAD_CHUNK), lambda bb, g, i: (bb, i, off + g))

    oc, *weights = pl.pallas_call(
        functools.partial(_attn_c_kernel, blocks=blocks, group=group, cast_every=cast_every),
        out_shape=[jax.ShapeDtypeStruct((b, s, C_WIDTH), _bf16)] +
                  [jax.ShapeDtypeStruct(w.shape[1:], _bf16) for w in ffn_weights],
        grid=(b, C_KV_HEADS, s // rows),
        in_specs=[
            _resident(sink_lanes.shape, lambda bb, g, i: (0, 0, 0)),
            pl.BlockSpec((1, rows, group_width), lambda bb, g, i: (bb, i, q_blk + g)),
            kv(0, True), kv(0, False), kv(C_KV_HEADS, True), kv(C_KV_HEADS, False),
        ] + [w_in_spec(w) for w in ffn_weights],
        out_specs=[pl.BlockSpec((1, rows, group_width), lambda bb, g, i: (bb, i, g))] +
                  [w_out_spec(w) for w in ffn_weights],
        compiler_params=pltpu.CompilerParams(
            dimension_semantics=("arbitrary", "arbitrary", "arbitrary"), vmem_limit_bytes=V7X_VMEM_LIMIT),
        name="attn_c",
    )(sink_lanes, rest, kv_rep, kv_rep, kv_rep, kv_rep, *ffn_weights)
    return oc.reshape(b * s, C_WIDTH), weights


def _out_proj_kernel(oa_ref, ob_ref, oc_ref, x_ref, w_ref, xo_ref):
    acc = jnp.dot(oa_ref[...], w_ref[0, 0:A_WIDTH, :], preferred_element_type=_f32)
    acc = acc + jnp.dot(ob_ref[...], w_ref[0, A_WIDTH:A_WIDTH + B_WIDTH, :], preferred_element_type=_f32)
    acc = acc + jnp.dot(oc_ref[...], w_ref[0, A_WIDTH + B_WIDTH:, :], preferred_element_type=_f32)
    xo_ref[...] = x_ref[...] + acc


def _out_proj(oa, ob, oc, x, w, layer, *, tm):
    m, d = x.shape
    assert m % tm == 0

    def rows(width):
        return pl.BlockSpec((tm, width), lambda i: (i, 0))

    return pl.pallas_call(
        _out_proj_kernel,
        out_shape=jax.ShapeDtypeStruct((m, d), _f32),
        grid=(m // tm,),
        in_specs=[rows(A_WIDTH), rows(B_WIDTH), rows(C_WIDTH), rows(d),
                  _resident((1,) + w.shape[1:], lambda i: (layer, 0, 0))],
        out_specs=rows(d),
        compiler_params=pltpu.CompilerParams(
            dimension_semantics=("parallel",), vmem_limit_bytes=V7X_VMEM_LIMIT),
        name="out_proj",
    )(oa, ob, oc, x, w)


def _ffn_kernel(x_ref, g_ref, wg_ref, wu_ref, wd_ref, o_ref, h_ref):
    f = pl.program_id(1)

    @pl.when(f == 0)
    def _():
        x = x_ref[...]
        ms = jnp.mean(x * x, axis=-1, keepdims=True)
        h_ref[...] = (x * lax.rsqrt(ms + EPS) * g_ref[...]).astype(_bf16)
        o_ref[...] = x

    h = h_ref[...]
    gate = jnp.dot(h, wg_ref[...], preferred_element_type=_f32)
    up = jnp.dot(h, wu_ref[...], preferred_element_type=_f32)
    act = (gate * _sigmoid(gate) * up).astype(_bf16)
    o_ref[...] += jnp.dot(act, wd_ref[...], preferred_element_type=_f32)


def _ffn(x, g, wg, wu, wd, *, tm, tf):
    m, d = x.shape
    ff = wg.shape[1]
    assert m % tm == 0 and ff % tf == 0
    return pl.pallas_call(
        _ffn_kernel,
        out_shape=jax.ShapeDtypeStruct((m, d), _f32),
        grid=(m // tm, ff // tf),
        in_specs=[
            pl.BlockSpec((tm, d), lambda i, f: (i, 0)),
            pl.BlockSpec((1, d), lambda i, f: (0, 0)),
            pl.BlockSpec((d, tf), lambda i, f: (0, f)),
            pl.BlockSpec((d, tf), lambda i, f: (0, f)),
            pl.BlockSpec((tf, d), lambda i, f: (f, 0)),
        ],
        out_specs=pl.BlockSpec((tm, d), lambda i, f: (i, 0)),
        scratch_shapes=[pltpu.VMEM((tm, d), _bf16)],
        compiler_params=pltpu.CompilerParams(
            dimension_semantics=("parallel", "arbitrary"), vmem_limit_bytes=V7X_VMEM_LIMIT),
        name="swiglu_ffn",
    )(x, g, wg, wu, wd)


def _tile(m, pref):
    t = pref
    while m % t:
        t //= 2
    return t


def _qk_gain_and_flag(a_q_g, a_k_g, c_q_g, c_k_g):
    scale = HEAD_DIM ** -0.5 * LOG2E
    ones = lambda n: jnp.ones((n,), _f32)
    zeros = lambda n: jnp.zeros((n,), _f32)
    gain = jnp.concatenate([
        jnp.tile(a_q_g * scale, A_HEADS), jnp.tile(a_k_g, A_HEADS), ones(A_WIDTH + 2 * B_WIDTH),
        jnp.tile(c_q_g * scale, C_Q_HEADS), jnp.tile(c_k_g, C_KV_HEADS), ones(C_KV_WIDTH)])
    flag = jnp.concatenate([ones(2 * A_WIDTH), zeros(A_WIDTH + 2 * B_WIDTH),
                            ones(C_WIDTH + C_KV_WIDTH), zeros(C_KV_WIDTH)])
    return gain.reshape(1, IN_WIDTH), flag.reshape(1, IN_WIDTH)


def kernel(x, norm1_g, w_in, a_q_g, a_k_g, conv_w, conv_b, conv_ln_g, conv_ln_b,
           c_q_g, c_k_g, c_sinks, w_out, norm2_g, w_gate, w_up, w_down):
    b, s, d = x.shape
    m = b * s
    depth = w_in.shape[0]
    lane = jnp.arange(HEAD_CHUNK) // HEAD_DIM
    head_ones = (lane[:, None] == lane[None, :]).astype(_bf16)
    dst = jnp.arange(KV_REP_WIDTH)
    kv_rep = (jnp.arange(HEAD_CHUNK)[:, None] ==
              (dst // HEAD_CHUNK) * HEAD_DIM + dst % HEAD_DIM).astype(_bf16)
    w_in, w_out = w_in.astype(_bf16), w_out.astype(_bf16)
    xf = x.reshape(m, d)
    for l in range(depth):
        gain, flag = _qk_gain_and_flag(a_q_g[l], a_k_g[l], c_q_g[l], c_k_g[l])
        a1, a4, a16, ob, rest, kvr = _in_proj(
            xf, norm1_g[l].reshape(1, d), w_in, l, head_ones, kv_rep, gain, flag,
            conv_w[l], conv_b[l].reshape(1, -1), conv_ln_g[l].reshape(1, -1), conv_ln_b[l].reshape(1, -1),
            batch=b, tm=_tile(s, 512))
        oa = _attn_a(a1.reshape(b, s, A_QKV_WIDTH), a4, a16, head_ones)
        rest = rest.reshape(b, s, REST_WIDTH)
        sink_lanes = jnp.repeat(c_sinks[l] * LOG2E, HEAD_DIM).reshape(C_WIDTH // HEAD_CHUNK, 1, HEAD_CHUNK)
        oc, (wg, wu, wd) = _attn_c(rest, kvr.reshape(b, s, KV_REP_WIDTH), sink_lanes, (w_gate, w_up, w_down), l,
                                   blocks=_tile(s // ATT_BLOCK, 4), group=2)
        xf = _out_proj(oa, ob, oc, xf, w_out, l, tm=_tile(m, 512))
        xf = _ffn(xf, norm2_g[l].reshape(1, d), wg, wu, wd, tm=_tile(m, 1024), tf=512)
    return xf.reshape(b, s, d)
```

```python
import functools
import math

import jax
import jax.numpy as jnp
from jax import lax
from jax.experimental import pallas as pl
from jax.experimental.pallas import tpu as pltpu

HEAD_DIM = 64
A_HEADS = 8
A_WIDTH = A_HEADS * HEAD_DIM
DILATED_PAIRS = ((128, 1), (512, 4), (2048, 16))
B_WIDTH = 512
CONV_WIDTH = 31
C_Q_HEADS = 16
C_KV_HEADS = 2
C_WIDTH = C_Q_HEADS * HEAD_DIM
C_KV_WIDTH = C_KV_HEADS * HEAD_DIM
C_WINDOW = 128
IN_WIDTH = 3 * A_WIDTH + 2 * B_WIDTH + C_WIDTH + 2 * C_KV_WIDTH
EPS = 1e-6
LOG2E = math.log2(math.e)

LANES = 128
SUBLANES = 8
BF16_ROWS = 16
ATT_BLOCK = 128
HEAD_CHUNK = 256
HEADS_PER_CHUNK = HEAD_CHUNK // HEAD_DIM
N_SLABS = HEAD_CHUNK // LANES
A_QKV_WIDTH = 3 * A_WIDTH
REST_WIDTH = C_WIDTH + 2 * C_KV_WIDTH
KV_REP_WIDTH = 2 * C_KV_HEADS * HEAD_CHUNK
CONV_HALO = 32
MASKED_SCORE = -1e30
V7X_VMEM_LIMIT = 56 * 1024 * 1024

_bf16 = jnp.bfloat16
_f32 = jnp.float32


def _sigmoid(v):
    return 1.0 / (1.0 + jnp.exp(-v))


def _resident(shape, index_map):
    return pl.BlockSpec(shape, index_map, pipeline_mode=pl.Buffered(1))


CONV_ROWS = 64


def _conv_tasks(h_halves, is_first, w_ref, b_ref, lg_ref, lb_ref, ob_ref, hext_ref, shift_ref, y_ref, *, tm):
    col_groups = [slice(cg * LANES, (cg + 1) * LANES) for cg in range(B_WIDTH // LANES)]
    shift_rows = tm + CONV_HALO - SUBLANES
    first = CONV_HALO - (CONV_WIDTH - 1)

    def fill(pace):
        tail = hext_ref[tm:tm + CONV_HALO, :]
        hext_ref[0:CONV_HALO, :] = jnp.where(is_first, 0.0, tail)
        for half, h in enumerate(h_halves):
            hext_ref[CONV_HALO:CONV_HALO + tm, half * HEAD_CHUNK:(half + 1) * HEAD_CHUNK] = h

    def shift(j, pace):
        for cs in col_groups:
            shift_ref[j - 1, :, cs] = hext_ref[j:j + shift_rows, cs]

    def taps(cs, r0, pace):
        acc = jnp.zeros((CONV_ROWS, LANES), _f32) + pace
        for k in range(CONV_WIDTH):
            a, j = divmod(first + k, SUBLANES)
            rows = slice(r0 + a * SUBLANES, r0 + a * SUBLANES + CONV_ROWS)
            tap = hext_ref[rows, cs] if j == 0 else shift_ref[j - 1, rows, cs]
            acc = acc + w_ref[k:k + 1, cs] * tap
        y_ref[r0:r0 + CONV_ROWS, cs] = acc + b_ref[:, cs]

    def finish(r0, pace):
        y = y_ref[r0:r0 + CONV_ROWS, :]
        mu = jnp.mean(y, axis=-1, keepdims=True)
        yc = y - mu
        var = jnp.mean(yc * yc, axis=-1, keepdims=True)
        z = yc * lax.rsqrt(var + EPS) * lg_ref[...] + lb_ref[...]
        ob_ref[r0:r0 + CONV_ROWS, :] = (z * _sigmoid(z)).astype(ob_ref.dtype)

    tasks = [fill] + [functools.partial(shift, j) for j in range(1, SUBLANES)]
    for r0 in range(0, tm, CONV_ROWS):
        tasks += [functools.partial(taps, cs, r0) for cs in col_groups]
        tasks.append(functools.partial(finish, r0))
    return tasks


def _in_proj_kernel(x_ref, g_ref, w_ref, e_ref, rep_ref, gain_ref, flag_ref, cw_ref, cb_ref, lg_ref, lb_ref,
                    wo_ref, a1_ref, a4_ref, a16_ref, ob_ref, rest_ref, kv_ref, wo_out,
                    h_ref, slab_ref, slab4_ref, hext_ref, shift_ref, y_ref, *, tm, tiles_per_seq):
    wo_out[...] = wo_ref[0].astype(wo_out.dtype)
    x = x_ref[...]
    ms = jnp.mean(x * x, axis=-1, keepdims=True)
    h_ref[...] = (x * lax.rsqrt(ms + EPS) * g_ref[...]).astype(_bf16)

    n_chunks = IN_WIDTH // HEAD_CHUNK
    a_chunks = A_QKV_WIDTH // HEAD_CHUNK
    qk_chunks = 2 * A_WIDTH // HEAD_CHUNK
    conv_chunks = 2 * B_WIDTH // HEAD_CHUNK
    cq_chunk = a_chunks + conv_chunks
    order = list(range(a_chunks, cq_chunk)) + list(range(a_chunks)) + list(range(cq_chunk, n_chunks))

    def project(chunk):
        return jnp.dot(h_ref[...], w_ref[:, chunk * HEAD_CHUNK:(chunk + 1) * HEAD_CHUNK],
                       preferred_element_type=_f32)

    conv_in, conv_todo = [], []
    a_next = project(order[0])
    for pos, chunk in enumerate(order):
        sl = slice(chunk * HEAD_CHUNK, (chunk + 1) * HEAD_CHUNK)
        a = a_next
        if pos + 1 < n_chunks:
            a_next = project(order[pos + 1])
        share = -(-len(conv_todo) // max(n_chunks - 1 - pos, 1))
        if share:
            bits = pltpu.bitcast(a[0:SUBLANES, 0:LANES], jnp.int32)
            pace = lax.shift_right_logical(lax.shift_right_logical(bits, 16), 16)[0:1, :].astype(_f32)
            for task in conv_todo[:share]:
                task(pace)
            del conv_todo[:share]
        if chunk < qk_chunks or chunk >= cq_chunk:
            ss = jnp.dot((a * a).astype(_bf16), e_ref[...], preferred_element_type=_f32)
            yn = a * lax.rsqrt(ss * (1.0 / HEAD_DIM) + EPS) * gain_ref[:, sl]
            a = jnp.where(flag_ref[:, sl] > 0.0, yn, a) if chunk == n_chunks - 1 else yn
        if chunk < a_chunks:
            a1_ref[:, sl] = a.astype(_bf16)
            base = (chunk % 2) * N_SLABS
            for s in range(N_SLABS):
                slab_ref[base + s] = a[:, s * LANES:(s + 1) * LANES]
            d4, d16 = DILATED_PAIRS[1][1], DILATED_PAIRS[2][1]
            for s in range(N_SLABS):
                cs = slice(chunk * HEAD_CHUNK + s * LANES, chunk * HEAD_CHUNK + (s + 1) * LANES)
                for r in range(d4):
                    stream = slab_ref[base + s, pl.ds(r, tm // d4, stride=d4), :]
                    a4_ref[0, r, :, cs] = stream.astype(_bf16)
                    slab4_ref[base + s, r] = stream
                for r in range(d16):
                    a16_ref[0, r, :, cs] = slab4_ref[base + s, r % d4,
                                                     pl.ds(r // d4, tm // d16, stride=d16 // d4), :].astype(_bf16)
        elif chunk < cq_chunk:
            conv_in.append(a)
            if len(conv_in) == conv_chunks:
                half = conv_chunks // 2
                glu = [conv_in[c] * _sigmoid(conv_in[half + c]) for c in range(half)]
                conv_todo = _conv_tasks(glu, pl.program_id(0) % tiles_per_seq == 0, cw_ref, cb_ref, lg_ref,
                                        lb_ref, ob_ref, hext_ref, shift_ref, y_ref, tm=tm)
        else:
            rsl = slice(sl.start - cq_chunk * HEAD_CHUNK, sl.stop - cq_chunk * HEAD_CHUNK)
            ab = a.astype(_bf16)
            rest_ref[:, rsl] = ab
            if chunk == n_chunks - 1:
                kv_ref[...] = jnp.dot(ab, rep_ref[...], preferred_element_type=_f32).astype(_bf16)


def _in_proj(x, g, w, head_ones, kv_rep, gain, flag, conv_w, conv_b, ln_g, ln_b, w_out, layer, *, batch, tm):
    m, d = x.shape
    s = m // batch
    n = w.shape[1]
    steps = m // tm
    assert n == IN_WIDTH and s % tm == 0 and (IN_WIDTH - 2 * C_KV_WIDTH) % HEAD_CHUNK == 0
    assert w_out.shape[1] % (steps * BF16_ROWS) == 0
    wo_rows = w_out.shape[1] // steps
    assert tm % CONV_ROWS == 0 and CONV_HALO >= CONV_WIDTH - 1 and CONV_HALO % SUBLANES == 0
    tiles_per_seq = s // tm

    def stream_spec(dil):
        return pl.BlockSpec((1, dil, tm // dil, A_QKV_WIDTH),
                            lambda i: (i // tiles_per_seq, 0, i % tiles_per_seq, 0))

    def rows(width):
        return pl.BlockSpec((tm, width), lambda i: (i, 0))

    def whole(arr):
        return _resident(arr.shape, lambda i: (0,) * arr.ndim)

    d4, d16 = DILATED_PAIRS[1][1], DILATED_PAIRS[2][1]
    return pl.pallas_call(
        functools.partial(_in_proj_kernel, tm=tm, tiles_per_seq=tiles_per_seq),
        out_shape=(jax.ShapeDtypeStruct((m, A_QKV_WIDTH), _bf16),
                   jax.ShapeDtypeStruct((batch, d4, s // d4, A_QKV_WIDTH), _bf16),
                   jax.ShapeDtypeStruct((batch, d16, s // d16, A_QKV_WIDTH), _bf16),
                   jax.ShapeDtypeStruct((m, B_WIDTH), _bf16),
                   jax.ShapeDtypeStruct((m, REST_WIDTH), _bf16),
                   jax.ShapeDtypeStruct((m, KV_REP_WIDTH), _bf16),
                   jax.ShapeDtypeStruct(w_out.shape[1:], _bf16)),
        grid=(steps,),
        in_specs=[
            rows(d),
            whole(g),
            whole(w),
            whole(head_ones), whole(kv_rep), whole(gain), whole(flag),
            whole(conv_w), whole(conv_b), whole(ln_g), whole(ln_b),
            pl.BlockSpec((1, wo_rows, w_out.shape[2]), lambda i: (layer, i, 0)),
        ],
        out_specs=(rows(A_QKV_WIDTH), stream_spec(d4), stream_spec(d16), rows(B_WIDTH), rows(REST_WIDTH),
                   rows(KV_REP_WIDTH), pl.BlockSpec((wo_rows, w_out.shape[2]), lambda i: (i, 0))),
        scratch_shapes=[pltpu.VMEM((tm, d), _bf16), pltpu.VMEM((2 * N_SLABS, tm, LANES), _f32),
                        pltpu.VMEM((2 * N_SLABS, d4, tm // d4, LANES), _f32),
                        pltpu.VMEM((CONV_HALO + tm, B_WIDTH), _f32),
                        pltpu.VMEM((SUBLANES - 1, tm + CONV_HALO - SUBLANES, B_WIDTH), _f32),
                        pltpu.VMEM((tm, B_WIDTH), _f32)],
        compiler_params=pltpu.CompilerParams(
            dimension_semantics=("arbitrary",), vmem_limit_bytes=V7X_VMEM_LIMIT),
        name="in_proj",
    )(x, g, w, head_ones, kv_rep, gain, flag, conv_w, conv_b, ln_g, ln_b, w_out)


def _per_head_lanes(cols):
    half = lax.broadcasted_iota(jnp.int32, (1, LANES), 1) < HEAD_DIM
    return jnp.concatenate([jnp.where(half, cols[0], cols[1]), jnp.where(half, cols[2], cols[3])], axis=1)


def _attn_consts():
    t = ATT_BLOCK
    lane_head = lax.broadcasted_iota(jnp.int32, (1, HEAD_CHUNK), 1) // HEAD_DIM
    head_mask = [(lane_head == h).astype(_bf16) for h in range(HEADS_PER_CHUNK)]
    lower = lax.broadcasted_iota(jnp.int32, (t, t), 1) <= lax.broadcasted_iota(jnp.int32, (t, t), 0)
    return head_mask, lower, lower.astype(_bf16)


def _attn_scores(units, consts, head_ones):
    head_mask = consts[0]
    scores, far_scores = [], []
    for (q, kp, kc, vp, vc, has_prev) in units:
        n = q.shape[1] // HEAD_CHUNK
        qs = jnp.concatenate([q[:, c * HEAD_CHUNK:(c + 1) * HEAD_CHUNK] * head_mask[h]
                              for c in range(n) for h in range(HEADS_PER_CHUNK)], axis=0)
        kcat = jnp.concatenate([kp, kc], axis=0)
        scores.append(lax.dot_general(qs, kcat, (((1,), (1,)), ((), ())), preferred_element_type=_f32))
    if head_ones is not None:
        for (q, kp, kc, vp, vc, has_prev) in units:
            far_scores.append(jnp.dot(q * kp, head_ones, preferred_element_type=_f32))
    return scores, far_scores


def _attn_finish(units, staged, consts, head_ones, sinks):
    t = ATT_BLOCK
    head_mask, lower, lower_bf = consts
    scores, far_scores = staged
    far_key = head_ones is not None
    outs = []
    for u, (q, kp, kc, vp, vc, has_prev) in enumerate(units):
        per_chunk = []
        for c in range(q.shape[1] // HEAD_CHUNK):
            ms, ls, pcs = [], [], []
            for h in range(HEADS_PER_CHUNK):
                base = (c * HEADS_PER_CHUNK + h) * t
                s2 = scores[u][base:base + t]
                s_prev = s2[:, :t]
                if has_prev is not True:
                    s_prev = jnp.where(has_prev, s_prev, MASKED_SCORE)
                s = jnp.where(lower, s2[:, t:], s_prev)
                m = jnp.max(s, axis=-1, keepdims=True)
                p = jnp.exp2(s - m)
                ls.append(jnp.sum(p, axis=-1, keepdims=True))
                ms.append(m)
                pb = p.astype(_bf16)
                p_cur = pb * lower_bf
                pcs.append(pb - p_cur)
                pcs.append(p_cur)
            m_all = _per_head_lanes(ms)
            l_all = _per_head_lanes(ls)
            rescale = extra_w = None
            if far_key or sinks is not None:
                if far_key:
                    es = far_scores[u]
                    if has_prev is not True:
                        es = jnp.where(has_prev, es, MASKED_SCORE)
                else:
                    es = sinks[u][c]
                m_new = jnp.maximum(m_all, es)
                rescale = jnp.exp2(m_all - m_new)
                extra_w = jnp.exp2(es - m_new)
                l_all = l_all * rescale + extra_w
                m_all = m_new
            per_chunk.append((jnp.concatenate(pcs, axis=1), m_all, l_all, rescale, extra_w))

        vcat = jnp.concatenate([vp, vc], axis=0)
        v_bd = jnp.concatenate([vcat * head_mask[h] for h in range(HEADS_PER_CHUNK)], axis=0)
        res = []
        for (p_all, m_all, l_all, rescale, extra_w) in per_chunk:
            acc = jnp.dot(p_all, v_bd, preferred_element_type=_f32)
            if rescale is not None:
                acc = acc * rescale
            if far_key:
                acc = acc + extra_w * vp.astype(_f32)
            res.append((acc, m_all, l_all))
        outs.append(res)
    return outs


def _attn_pipeline(groups, *, head_ones=None):
    consts = _attn_consts()
    pending = None
    for grp in list(groups) + [None]:
        nxt = None
        if grp is not None:
            units = grp[0]()
            nxt = (grp, units, _attn_scores(units, consts, head_ones))
        if pending is not None:
            (_, sinks, consume), units_p, staged = pending
            consume(_attn_finish(units_p, staged, consts, head_ones, sinks))
        pending = nxt


def _merge(old, new):
    acc_o, m_o, l_o = old
    acc_n, m_n, l_n = new
    m = jnp.maximum(m_o, m_n)
    w_o = jnp.exp2(m_o - m)
    w_n = jnp.exp2(m_n - m)
    return w_o * acc_o + w_n * acc_n, m, w_o * l_o + w_n * l_n


def _attn_a_kernel(e_ref, q1, k1p, k1c, v1p, v1c, q4, k4p, k4c, v4p, v4c, q16, k16p, k16c, v16p, v16c,
                   o_ref, acc_ref, m_ref, l_ref, *, d4, d16):
    t = ATT_BLOCK
    has_prev = pl.program_id(1) > 0
    group16, group = 2, 4
    blocks4 = d16 // d4
    blocks1 = d16
    state = (acc_ref, m_ref, l_ref)

    def slab(x, s):
        return x[:, s * LANES:(s + 1) * LANES]

    def load_state(rows):
        return tuple(jnp.concatenate([ref[s, rows, :] for s in range(N_SLABS)], axis=1) for ref in state)

    def store_state(rows, vals):
        for ref, val in zip(state, vals):
            for s in range(N_SLABS):
                ref[s, rows, :] = slab(val, s)

    head_ones = e_ref[...]

    def body16(rg, carry):
        rs = [group16 * rg + u for u in range(group16)]

        def make():
            return [(q16[0, r], k16p[0, r], k16c[0, r], v16p[0, r], v16c[0, r], has_prev) for r in rs]

        def consume(outs):
            for r, res in zip(rs, outs):
                store_state(pl.ds(r, t, stride=d16), res[0])

        _attn_pipeline([(make, None, consume)], head_ones=head_ones)
        return carry

    lax.fori_loop(0, d16 // group16, body16, 0)

    def stream_units(base, q, kp0, kc, vp0, vc):
        units = []
        for bl in range(group):
            start = pl.multiple_of(base + bl * t, t)
            if bl == 0:
                before = pl.multiple_of(jnp.maximum(base - t, 0), t)
                first = base == 0
                kp = jnp.where(first, kp0, kc[pl.ds(before, t), :])
                vp = jnp.where(first, vp0, vc[pl.ds(before, t), :])
                ok = jnp.logical_or(has_prev, base > 0)
            else:
                before = pl.multiple_of(base + (bl - 1) * t, t)
                kp, vp, ok = kc[pl.ds(before, t), :], vc[pl.ds(before, t), :], True
            units.append((q[pl.ds(start, t), :], kp, kc[pl.ds(start, t), :], vp, vc[pl.ds(start, t), :], ok))
        return units

    groups4 = blocks4 // group

    def body4(i, carry):
        r = i // groups4
        base = pl.multiple_of((i % groups4) * (group * t), group * t)

        def make():
            return stream_units(base, q4.at[0, r], k4p[0, r], k4c.at[0, r], v4p[0, r], v4c.at[0, r])

        def consume(outs):
            for bl, res in enumerate(outs):
                rows = pl.ds((base + bl * t) * d4 + r, t, stride=d4)
                store_state(rows, _merge(load_state(rows), res[0]))

        _attn_pipeline([(make, None, consume)], head_ones=head_ones)
        return carry

    lax.fori_loop(0, d4 * groups4, body4, 0)

    def body1(g, carry):
        base = pl.multiple_of(g * (group * t), group * t)

        def make():
            return stream_units(base, q1.at[0], k1p[0], k1c.at[0], v1p[0], v1c.at[0])

        def consume(outs):
            for bl, res in enumerate(outs):
                rows = pl.ds(pl.multiple_of(base + bl * t, t), t)
                acc, _, l = _merge(load_state(rows), res[0])
                o_ref[0, rows, :] = (acc * (1.0 / l)).astype(o_ref.dtype)

        _attn_pipeline([(make, None, consume)], head_ones=head_ones)
        return carry

    lax.fori_loop(0, blocks1 // group, body1, 0)


def _attn_a(a1, a4, a16, head_ones):
    b, s, _ = a1.shape
    t = ATT_BLOCK
    (w1, d1), (w4, d4), (w16, d16) = DILATED_PAIRS
    assert d1 == 1 and w1 == t and w4 == t * d4 and w16 == t * d16 and d16 % d4 == 0
    sup = t * d16
    assert s % sup == 0
    n_chunks = A_WIDTH // HEAD_CHUNK
    qc, kc, vc = 0, n_chunks, 2 * n_chunks

    def nat(rows, per_sup, off, prev):
        def imap(bb, n, c):
            blk = jnp.maximum(n * per_sup - 1, 0) if prev else n
            return (bb, blk, off + c)
        return pl.BlockSpec((1, rows, HEAD_CHUNK), imap)

    def strm(d, rows, per_sup, off, prev):
        def imap(bb, n, c):
            blk = jnp.maximum(n * per_sup - 1, 0) if prev else n
            return (bb, 0, blk, off + c)
        return pl.BlockSpec((1, d, rows, HEAD_CHUNK), imap)

    in_specs = [_resident((HEAD_CHUNK, HEAD_CHUNK), lambda bb, n, c: (0, 0)),
                nat(sup, 1, qc, False),
                nat(t, sup // t, kc, True), nat(sup, 1, kc, False),
                nat(t, sup // t, vc, True), nat(sup, 1, vc, False),
                strm(d4, sup // d4, 1, qc, False),
                strm(d4, t, sup // d4 // t, kc, True), strm(d4, sup // d4, 1, kc, False),
                strm(d4, t, sup // d4 // t, vc, True), strm(d4, sup // d4, 1, vc, False),
                strm(d16, t, 1, qc, False),
                strm(d16, t, 1, kc, True), strm(d16, t, 1, kc, False),
                strm(d16, t, 1, vc, True), strm(d16, t, 1, vc, False)]
    state = pltpu.VMEM((N_SLABS, sup, LANES), _f32)
    return pl.pallas_call(
        functools.partial(_attn_a_kernel, d4=d4, d16=d16),
        out_shape=jax.ShapeDtypeStruct((b, s, A_WIDTH), _bf16),
        grid=(b, s // sup, n_chunks),
        in_specs=in_specs,
        out_specs=pl.BlockSpec((1, sup, HEAD_CHUNK), lambda bb, n, c: (bb, n, c)),
        scratch_shapes=[state, state, state],
        compiler_params=pltpu.CompilerParams(
            dimension_semantics=("parallel", "parallel", "parallel"), vmem_limit_bytes=V7X_VMEM_LIMIT),
        name="attn_a",
    )(head_ones, a1, a1, a1, a1, a1, a4, a4, a4, a4, a4, a16, a16, a16, a16, a16).reshape(b * s, A_WIDTH)


def _attn_c_kernel(sink_ref, q_ref, kp_ref, kc_ref, vp_ref, vc_ref, wg_ref, wu_ref,
                   o_ref, wg_out, wu_out, *, blocks, group):
    t = ATT_BLOCK
    g = pl.program_id(1)
    has_prev = pl.program_id(2) > 0
    wg_out[...] = wg_ref[0].astype(wg_out.dtype)
    wu_out[...] = wu_ref[0].astype(wu_out.dtype)
    n = C_WIDTH // C_KV_HEADS // HEAD_CHUNK
    sink_rows = [sink_ref[g * n + c] for c in range(n)]
    groups = []
    for b0 in range(0, blocks, group):
        def make(b0=b0):
            units = []
            for blk in range(b0, b0 + group):
                cur = slice(blk * t, (blk + 1) * t)
                if blk == 0:
                    kp, vp, ok = kp_ref[0], vp_ref[0], has_prev
                else:
                    before = slice((blk - 1) * t, blk * t)
                    kp, vp, ok = kc_ref[0, before], vc_ref[0, before], True
                units.append((q_ref[0, cur], kp, kc_ref[0, cur], vp, vc_ref[0, cur], ok))
            return units

        def consume(outs, b0=b0):
            for blk, res in zip(range(b0, b0 + group), outs):
                for c, (acc, _, l) in enumerate(res):
                    o_ref[0, blk * t:(blk + 1) * t, c * HEAD_CHUNK:(c + 1) * HEAD_CHUNK] = (
                        acc * (1.0 / l)).astype(o_ref.dtype)

        groups.append((make, [sink_rows] * group, consume))
    _attn_pipeline(groups)


def _attn_c(rest, kv_rep, sink_lanes, cast_weights, layer, *, blocks, group):
    b, s, _ = rest.shape
    t = ATT_BLOCK
    rows = blocks * t
    steps = b * C_KV_HEADS * (s // rows)
    assert all(w.shape[1] % (steps * BF16_ROWS) == 0 for w in cast_weights)

    def w_in_spec(w):
        return pl.BlockSpec((1, w.shape[1] // steps, w.shape[2]),
                            lambda bb, g, i: (layer, (bb * C_KV_HEADS + g) * (s // rows) + i, 0))

    def w_out_spec(w):
        return pl.BlockSpec((w.shape[1] // steps, w.shape[2]),
                            lambda bb, g, i: ((bb * C_KV_HEADS + g) * (s // rows) + i, 0))
    group_width = C_WIDTH // C_KV_HEADS
    q_off = 0
    assert C_WINDOW == t and s % rows == 0 and q_off % group_width == 0 and blocks % group == 0
    q_blk = q_off // group_width

    def kv(off, prev):
        if prev:
            return pl.BlockSpec((1, t, HEAD_CHUNK), lambda bb, g, i: (bb, jnp.maximum(i * blocks - 1, 0), off + g))
        return pl.BlockSpec((1, rows, HEAD_CHUNK), lambda bb, g, i: (bb, i, off + g))

    oc, *weights = pl.pallas_call(
        functools.partial(_attn_c_kernel, blocks=blocks, group=group),
        out_shape=[jax.ShapeDtypeStruct((b, s, C_WIDTH), _bf16)] +
                  [jax.ShapeDtypeStruct(w.shape[1:], _bf16) for w in cast_weights],
        grid=(b, C_KV_HEADS, s // rows),
        in_specs=[
            _resident(sink_lanes.shape, lambda bb, g, i: (0, 0, 0)),
            pl.BlockSpec((1, rows, group_width), lambda bb, g, i: (bb, i, q_blk + g)),
            kv(0, True), kv(0, False), kv(C_KV_HEADS, True), kv(C_KV_HEADS, False),
        ] + [w_in_spec(w) for w in cast_weights],
        out_specs=[pl.BlockSpec((1, rows, group_width), lambda bb, g, i: (bb, i, g))] +
                  [w_out_spec(w) for w in cast_weights],
        compiler_params=pltpu.CompilerParams(
            dimension_semantics=("parallel", "parallel", "parallel"), vmem_limit_bytes=V7X_VMEM_LIMIT),
        name="attn_c",
    )(sink_lanes, rest, kv_rep, kv_rep, kv_rep, kv_rep, *cast_weights)
    return oc.reshape(b * s, C_WIDTH), weights


def _out_proj_kernel(oa_ref, ob_ref, oc_ref, x_ref, w_ref, *cast_refs):
    xo_ref = cast_refs[len(cast_refs) // 2]
    for src, dst in zip(cast_refs[:len(cast_refs) // 2], cast_refs[len(cast_refs) // 2 + 1:]):
        dst[...] = src[0].astype(dst.dtype)
    acc = jnp.dot(oa_ref[...], w_ref[0:A_WIDTH, :], preferred_element_type=_f32)
    acc = acc + jnp.dot(ob_ref[...], w_ref[A_WIDTH:A_WIDTH + B_WIDTH, :], preferred_element_type=_f32)
    acc = acc + jnp.dot(oc_ref[...], w_ref[A_WIDTH + B_WIDTH:, :], preferred_element_type=_f32)
    xo_ref[...] = x_ref[...] + acc


def _out_proj(oa, ob, oc, x, w, casts, *, tm):
    m, d = x.shape
    steps = m // tm
    assert m % tm == 0 and all(cw.shape[1] % (steps * BF16_ROWS) == 0 for cw, _ in casts)

    def cast_in(cw, layer):
        return pl.BlockSpec((1, cw.shape[1] // steps, cw.shape[2]), lambda i: (layer, i, 0))

    def cast_out(cw):
        return pl.BlockSpec((cw.shape[1] // steps, cw.shape[2]), lambda i: (i, 0))

    def rows(width):
        return pl.BlockSpec((tm, width), lambda i: (i, 0))

    return pl.pallas_call(
        _out_proj_kernel,
        out_shape=[jax.ShapeDtypeStruct((m, d), _f32)] +
                  [jax.ShapeDtypeStruct(cw.shape[1:], _bf16) for cw, _ in casts],
        grid=(steps,),
        in_specs=[rows(A_WIDTH), rows(B_WIDTH), rows(C_WIDTH), rows(d),
                  _resident(w.shape, lambda i: (0, 0))] + [cast_in(cw, layer) for cw, layer in casts],
        out_specs=[rows(d)] + [cast_out(cw) for cw, _ in casts],
        compiler_params=pltpu.CompilerParams(
            dimension_semantics=("parallel",), vmem_limit_bytes=V7X_VMEM_LIMIT),
        name="out_proj",
    )(oa, ob, oc, x, w, *(cw for cw, _ in casts))


def _ffn_kernel(x_ref, g_ref, wg_ref, wu_ref, wd_ref, o_ref, h_ref):
    f = pl.program_id(1)

    @pl.when(f == 0)
    def _():
        x = x_ref[...]
        ms = jnp.mean(x * x, axis=-1, keepdims=True)
        h_ref[...] = (x * lax.rsqrt(ms + EPS) * g_ref[...]).astype(_bf16)
        o_ref[...] = x

    h = h_ref[...]
    gate = jnp.dot(h, wg_ref[...], preferred_element_type=_f32)
    up = jnp.dot(h, wu_ref[...], preferred_element_type=_f32)
    act = (gate * _sigmoid(gate) * up).astype(_bf16)
    o_ref[...] += jnp.dot(act, wd_ref[...], preferred_element_type=_f32)


def _ffn(x, g, wg, wu, wd, *, tm, tf):
    m, d = x.shape
    ff = wg.shape[1]
    assert m % tm == 0 and ff % tf == 0
    return pl.pallas_call(
        _ffn_kernel,
        out_shape=jax.ShapeDtypeStruct((m, d), _f32),
        grid=(m // tm, ff // tf),
        in_specs=[
            pl.BlockSpec((tm, d), lambda i, f: (i, 0)),
            pl.BlockSpec((1, d), lambda i, f: (0, 0)),
            pl.BlockSpec((d, tf), lambda i, f: (0, f)),
            pl.BlockSpec((d, tf), lambda i, f: (0, f)),
            pl.BlockSpec((tf, d), lambda i, f: (f, 0)),
        ],
        out_specs=pl.BlockSpec((tm, d), lambda i, f: (i, 0)),
        scratch_shapes=[pltpu.VMEM((tm, d), _bf16)],
        compiler_params=pltpu.CompilerParams(
            dimension_semantics=("parallel", "arbitrary"), vmem_limit_bytes=V7X_VMEM_LIMIT),
        name="swiglu_ffn",
    )(x, g, wg, wu, wd)


def _tile(m, pref):
    t = pref
    while m % t:
        t //= 2
    return t


def _qk_gain_and_flag(a_q_g, a_k_g, c_q_g, c_k_g):
    scale = HEAD_DIM ** -0.5 * LOG2E
    ones = lambda n: jnp.ones((n,), _f32)
    zeros = lambda n: jnp.zeros((n,), _f32)
    gain = jnp.concatenate([
        jnp.tile(a_q_g * scale, A_HEADS), jnp.tile(a_k_g, A_HEADS), ones(A_WIDTH + 2 * B_WIDTH),
        jnp.tile(c_q_g * scale, C_Q_HEADS), jnp.tile(c_k_g, C_KV_HEADS), ones(C_KV_WIDTH)])
    flag = jnp.concatenate([ones(2 * A_WIDTH), zeros(A_WIDTH + 2 * B_WIDTH),
                            ones(C_WIDTH + C_KV_WIDTH), zeros(C_KV_WIDTH)])
    return gain.reshape(1, IN_WIDTH), flag.reshape(1, IN_WIDTH)


def kernel(x, norm1_g, w_in, a_q_g, a_k_g, conv_w, conv_b, conv_ln_g, conv_ln_b,
           c_q_g, c_k_g, c_sinks, w_out, norm2_g, w_gate, w_up, w_down):
    b, s, d = x.shape
    m = b * s
    depth = w_in.shape[0]
    lane = jnp.arange(HEAD_CHUNK) // HEAD_DIM
    head_ones = (lane[:, None] == lane[None, :]).astype(_bf16)
    dst = jnp.arange(KV_REP_WIDTH)
    kv_rep = (jnp.arange(HEAD_CHUNK)[:, None] ==
              (dst // HEAD_CHUNK) * HEAD_DIM + dst % HEAD_DIM).astype(_bf16)
    w_in_l = w_in[0].astype(_bf16)
    xf = x.reshape(m, d)
    for l in range(depth):
        gain, flag = _qk_gain_and_flag(a_q_g[l], a_k_g[l], c_q_g[l], c_k_g[l])
        a1, a4, a16, ob, rest, kvr, w_out_l = _in_proj(
            xf, norm1_g[l].reshape(1, d), w_in_l, head_ones, kv_rep, gain, flag,
            conv_w[l], conv_b[l].reshape(1, -1), conv_ln_g[l].reshape(1, -1), conv_ln_b[l].reshape(1, -1),
            w_out, l, batch=b, tm=_tile(s, 512))
        oa = _attn_a(a1.reshape(b, s, A_QKV_WIDTH), a4, a16, head_ones)
        rest = rest.reshape(b, s, REST_WIDTH)
        sink_lanes = jnp.repeat(c_sinks[l] * LOG2E, HEAD_DIM).reshape(C_WIDTH // HEAD_CHUNK, 1, HEAD_CHUNK)
        oc, (wg, wu) = _attn_c(rest, kvr.reshape(b, s, KV_REP_WIDTH), sink_lanes, (w_gate, w_up), l,
                               blocks=_tile(s // ATT_BLOCK, 4), group=2)
        casts = [(w_down, l)] + ([(w_in, l + 1)] if l + 1 < depth else [])
        xf, wd, *nxt = _out_proj(oa, ob, oc, xf, w_out_l, casts, tm=_tile(m, 512))
        if nxt:
            w_in_l = nxt[0]
        xf = _ffn(xf, norm2_g[l].reshape(1, d), wg, wu, wd, tm=_tile(m, 1024), tf=512)
    return xf.reshape(b, s, d)
```

```python
import functools
import math

import jax
import jax.numpy as jnp
from jax import lax
from jax.experimental import pallas as pl
from jax.experimental.pallas import tpu as pltpu

HEAD_DIM = 64
A_HEADS = 8
A_WIDTH = A_HEADS * HEAD_DIM
DILATED_PAIRS = ((128, 1), (512, 4), (2048, 16))
B_WIDTH = 512
CONV_WIDTH = 31
C_Q_HEADS = 16
C_KV_HEADS = 2
C_WIDTH = C_Q_HEADS * HEAD_DIM
C_KV_WIDTH = C_KV_HEADS * HEAD_DIM
C_WINDOW = 128
IN_WIDTH = 3 * A_WIDTH + 2 * B_WIDTH + C_WIDTH + 2 * C_KV_WIDTH
EPS = 1e-6
LOG2E = math.log2(math.e)

LANES = 128
SUBLANES = 8
BF16_ROWS = 16
ATT_BLOCK = 128
HEAD_CHUNK = 256
HEADS_PER_CHUNK = HEAD_CHUNK // HEAD_DIM
N_SLABS = HEAD_CHUNK // LANES
A_QKV_WIDTH = 3 * A_WIDTH
REST_WIDTH = C_WIDTH + 2 * C_KV_WIDTH
KV_REP_WIDTH = 2 * C_KV_HEADS * HEAD_CHUNK
CONV_HALO = 32
MASKED_SCORE = -1e30
V7X_VMEM_LIMIT = 56 * 1024 * 1024

_bf16 = jnp.bfloat16
_f32 = jnp.float32


def _sigmoid(v):
    return 1.0 / (1.0 + jnp.exp(-v))


def _resident(shape, index_map):
    return pl.BlockSpec(shape, index_map, pipeline_mode=pl.Buffered(1))


CONV_ROWS = 64


def _conv_tasks(h_halves, is_first, w_ref, b_ref, lg_ref, lb_ref, ob_ref, hext_ref, shift_ref, y_ref, *, tm):
    col_groups = [slice(cg * LANES, (cg + 1) * LANES) for cg in range(B_WIDTH // LANES)]
    shift_rows = tm + CONV_HALO - SUBLANES
    first = CONV_HALO - (CONV_WIDTH - 1)

    def fill(pace):
        tail = hext_ref[tm:tm + CONV_HALO, :]
        hext_ref[0:CONV_HALO, :] = jnp.where(is_first, 0.0, tail)
        for half, h in enumerate(h_halves):
            hext_ref[CONV_HALO:CONV_HALO + tm, half * HEAD_CHUNK:(half + 1) * HEAD_CHUNK] = h

    def shift(j, pace):
        for cs in col_groups:
            shift_ref[j - 1, :, cs] = hext_ref[j:j + shift_rows, cs]

    def taps(cs, r0, pace):
        acc = jnp.zeros((CONV_ROWS, LANES), _f32) + pace
        for k in range(CONV_WIDTH):
            a, j = divmod(first + k, SUBLANES)
            rows = slice(r0 + a * SUBLANES, r0 + a * SUBLANES + CONV_ROWS)
            tap = hext_ref[rows, cs] if j == 0 else shift_ref[j - 1, rows, cs]
            acc = acc + w_ref[k:k + 1, cs] * tap
        y_ref[r0:r0 + CONV_ROWS, cs] = acc + b_ref[:, cs]

    def finish(r0, pace):
        y = y_ref[r0:r0 + CONV_ROWS, :]
        mu = jnp.mean(y, axis=-1, keepdims=True)
        yc = y - mu
        var = jnp.mean(yc * yc, axis=-1, keepdims=True)
        z = yc * lax.rsqrt(var + EPS) * lg_ref[...] + lb_ref[...]
        ob_ref[r0:r0 + CONV_ROWS, :] = (z * _sigmoid(z)).astype(ob_ref.dtype)

    tasks = [fill] + [functools.partial(shift, j) for j in range(1, SUBLANES)]
    for r0 in range(0, tm, CONV_ROWS):
        tasks += [functools.partial(taps, cs, r0) for cs in col_groups]
        tasks.append(functools.partial(finish, r0))
    return tasks


def _in_proj_kernel(x_ref, g_ref, w_ref, e_ref, rep_ref, gain_ref, flag_ref, cw_ref, cb_ref, lg_ref, lb_ref,
                    wo_ref, a1_ref, a4_ref, a16_ref, ob_ref, rest_ref, kv_ref, wo_out,
                    h_ref, slab_ref, slab4_ref, hext_ref, shift_ref, y_ref, *, tm, tiles_per_seq):
    wo_out[...] = wo_ref[0].astype(wo_out.dtype)
    x = x_ref[...]
    ms = jnp.mean(x * x, axis=-1, keepdims=True)
    h_ref[...] = (x * lax.rsqrt(ms + EPS) * g_ref[...]).astype(_bf16)

    n_chunks = IN_WIDTH // HEAD_CHUNK
    a_chunks = A_QKV_WIDTH // HEAD_CHUNK
    qk_chunks = 2 * A_WIDTH // HEAD_CHUNK
    conv_chunks = 2 * B_WIDTH // HEAD_CHUNK
    cq_chunk = a_chunks + conv_chunks
    order = list(range(a_chunks, cq_chunk)) + list(range(a_chunks)) + list(range(cq_chunk, n_chunks))

    def project(chunk):
        return jnp.dot(h_ref[...], w_ref[:, chunk * HEAD_CHUNK:(chunk + 1) * HEAD_CHUNK],
                       preferred_element_type=_f32)

    conv_in, conv_todo = [], []
    a_next = project(order[0])
    for pos, chunk in enumerate(order):
        sl = slice(chunk * HEAD_CHUNK, (chunk + 1) * HEAD_CHUNK)
        a = a_next
        if pos + 1 < n_chunks:
            a_next = project(order[pos + 1])
        share = -(-len(conv_todo) // max(n_chunks - 1 - pos, 1))
        if share:
            bits = pltpu.bitcast(a[0:SUBLANES, 0:LANES], jnp.int32)
            pace = lax.shift_right_logical(lax.shift_right_logical(bits, 16), 16)[0:1, :].astype(_f32)
            for task in conv_todo[:share]:
                task(pace)
            del conv_todo[:share]
        if chunk < qk_chunks or chunk >= cq_chunk:
            ss = jnp.dot((a * a).astype(_bf16), e_ref[...], preferred_element_type=_f32)
            yn = a * lax.rsqrt(ss * (1.0 / HEAD_DIM) + EPS) * gain_ref[:, sl]
            a = jnp.where(flag_ref[:, sl] > 0.0, yn, a) if chunk == n_chunks - 1 else yn
        if chunk < a_chunks:
            a1_ref[:, sl] = a.astype(_bf16)
            base = (chunk % 2) * N_SLABS
            for s in range(N_SLABS):
                slab_ref[base + s] = a[:, s * LANES:(s + 1) * LANES]
            d4, d16 = DILATED_PAIRS[1][1], DILATED_PAIRS[2][1]
            for s in range(N_SLABS):
                cs = slice(chunk * HEAD_CHUNK + s * LANES, chunk * HEAD_CHUNK + (s + 1) * LANES)
                for r in range(d4):
                    stream = slab_ref[base + s, pl.ds(r, tm // d4, stride=d4), :]
                    a4_ref[0, r, :, cs] = stream.astype(_bf16)
                    slab4_ref[base + s, r] = stream
                for r in range(d16):
                    a16_ref[0, r, :, cs] = slab4_ref[base + s, r % d4,
                                                     pl.ds(r // d4, tm // d16, stride=d16 // d4), :].astype(_bf16)
        elif chunk < cq_chunk:
            conv_in.append(a)
            if len(conv_in) == conv_chunks:
                half = conv_chunks // 2
                glu = [conv_in[c] * _sigmoid(conv_in[half + c]) for c in range(half)]
                conv_todo = _conv_tasks(glu, pl.program_id(0) % tiles_per_seq == 0, cw_ref, cb_ref, lg_ref,
                                        lb_ref, ob_ref, hext_ref, shift_ref, y_ref, tm=tm)
        else:
            rsl = slice(sl.start - cq_chunk * HEAD_CHUNK, sl.stop - cq_chunk * HEAD_CHUNK)
            ab = a.astype(_bf16)
            rest_ref[:, rsl] = ab
            if chunk == n_chunks - 1:
                kv_ref[...] = jnp.dot(ab, rep_ref[...], preferred_element_type=_f32).astype(_bf16)


def _in_proj(x, g, w, head_ones, kv_rep, gain, flag, conv_w, conv_b, ln_g, ln_b, w_out, layer, *, batch, tm):
    m, d = x.shape
    s = m // batch
    n = w.shape[1]
    steps = m // tm
    assert n == IN_WIDTH and s % tm == 0 and (IN_WIDTH - 2 * C_KV_WIDTH) % HEAD_CHUNK == 0
    assert w_out.shape[1] % (steps * BF16_ROWS) == 0
    wo_rows = w_out.shape[1] // steps
    assert tm % CONV_ROWS == 0 and CONV_HALO >= CONV_WIDTH - 1 and CONV_HALO % SUBLANES == 0
    tiles_per_seq = s // tm

    def stream_spec(dil):
        return pl.BlockSpec((1, dil, tm // dil, A_QKV_WIDTH),
                            lambda i: (i // tiles_per_seq, 0, i % tiles_per_seq, 0))

    def rows(width):
        return pl.BlockSpec((tm, width), lambda i: (i, 0))

    def whole(arr):
        return _resident(arr.shape, lambda i: (0,) * arr.ndim)

    d4, d16 = DILATED_PAIRS[1][1], DILATED_PAIRS[2][1]
    return pl.pallas_call(
        functools.partial(_in_proj_kernel, tm=tm, tiles_per_seq=tiles_per_seq),
        out_shape=(jax.ShapeDtypeStruct((m, A_QKV_WIDTH), _bf16),
                   jax.ShapeDtypeStruct((batch, d4, s // d4, A_QKV_WIDTH), _bf16),
                   jax.ShapeDtypeStruct((batch, d16, s // d16, A_QKV_WIDTH), _bf16),
                   jax.ShapeDtypeStruct((m, B_WIDTH), _bf16),
                   jax.ShapeDtypeStruct((m, REST_WIDTH), _bf16),
                   jax.ShapeDtypeStruct((m, KV_REP_WIDTH), _bf16),
                   jax.ShapeDtypeStruct(w_out.shape[1:], _bf16)),
        grid=(steps,),
        in_specs=[
            rows(d),
            whole(g),
            whole(w),
            whole(head_ones), whole(kv_rep), whole(gain), whole(flag),
            whole(conv_w), whole(conv_b), whole(ln_g), whole(ln_b),
            pl.BlockSpec((1, wo_rows, w_out.shape[2]), lambda i: (layer, i, 0)),
        ],
        out_specs=(rows(A_QKV_WIDTH), stream_spec(d4), stream_spec(d16), rows(B_WIDTH), rows(REST_WIDTH),
                   rows(KV_REP_WIDTH), pl.BlockSpec((wo_rows, w_out.shape[2]), lambda i: (i, 0))),
        scratch_shapes=[pltpu.VMEM((tm, d), _bf16), pltpu.VMEM((2 * N_SLABS, tm, LANES), _f32),
                        pltpu.VMEM((2 * N_SLABS, d4, tm // d4, LANES), _f32),
                        pltpu.VMEM((CONV_HALO + tm, B_WIDTH), _f32),
                        pltpu.VMEM((SUBLANES - 1, tm + CONV_HALO - SUBLANES, B_WIDTH), _f32),
                        pltpu.VMEM((tm, B_WIDTH), _f32)],
        compiler_params=pltpu.CompilerParams(
            dimension_semantics=("arbitrary",), vmem_limit_bytes=V7X_VMEM_LIMIT),
        name="in_proj",
    )(x, g, w, head_ones, kv_rep, gain, flag, conv_w, conv_b, ln_g, ln_b, w_out)


def _per_head_lanes(cols):
    half = lax.broadcasted_iota(jnp.int32, (1, LANES), 1) < HEAD_DIM
    return jnp.concatenate([jnp.where(half, cols[0], cols[1]), jnp.where(half, cols[2], cols[3])], axis=1)


def _attn_consts():
    t = ATT_BLOCK
    lane_head = lax.broadcasted_iota(jnp.int32, (1, HEAD_CHUNK), 1) // HEAD_DIM
    head_mask = [(lane_head == h).astype(_bf16) for h in range(HEADS_PER_CHUNK)]
    lower = lax.broadcasted_iota(jnp.int32, (t, t), 1) <= lax.broadcasted_iota(jnp.int32, (t, t), 0)
    return head_mask, lower, lower.astype(_bf16)


def _attn_scores(units, consts, head_ones):
    head_mask = consts[0]
    scores, far_scores = [], []
    for (q, kp, kc, vp, vc, has_prev) in units:
        n = q.shape[1] // HEAD_CHUNK
        qs = jnp.concatenate([q[:, c * HEAD_CHUNK:(c + 1) * HEAD_CHUNK] * head_mask[h]
                              for c in range(n) for h in range(HEADS_PER_CHUNK)], axis=0)
        kcat = jnp.concatenate([kp, kc], axis=0)
        scores.append(lax.dot_general(qs, kcat, (((1,), (1,)), ((), ())), preferred_element_type=_f32))
    if head_ones is not None:
        for (q, kp, kc, vp, vc, has_prev) in units:
            far_scores.append(jnp.dot(q * kp, head_ones, preferred_element_type=_f32))
    return scores, far_scores


def _attn_finish(units, staged, consts, head_ones, sinks):
    t = ATT_BLOCK
    head_mask, lower, lower_bf = consts
    scores, far_scores = staged
    far_key = head_ones is not None
    outs = []
    for u, (q, kp, kc, vp, vc, has_prev) in enumerate(units):
        per_chunk = []
        for c in range(q.shape[1] // HEAD_CHUNK):
            ms, ls, pcs = [], [], []
            for h in range(HEADS_PER_CHUNK):
                base = (c * HEADS_PER_CHUNK + h) * t
                s2 = scores[u][base:base + t]
                s_prev = s2[:, :t]
                if has_prev is not True:
                    s_prev = jnp.where(has_prev, s_prev, MASKED_SCORE)
                s = jnp.where(lower, s2[:, t:], s_prev)
                m = jnp.max(s, axis=-1, keepdims=True)
                p = jnp.exp2(s - m)
                ls.append(jnp.sum(p, axis=-1, keepdims=True))
                ms.append(m)
                pb = p.astype(_bf16)
                p_cur = pb * lower_bf
                pcs.append(pb - p_cur)
                pcs.append(p_cur)
            m_all = _per_head_lanes(ms)
            l_all = _per_head_lanes(ls)
            rescale = extra_w = None
            if far_key or sinks is not None:
                if far_key:
                    es = far_scores[u]
                    if has_prev is not True:
                        es = jnp.where(has_prev, es, MASKED_SCORE)
                else:
                    es = sinks[u][c]
                m_new = jnp.maximum(m_all, es)
                rescale = jnp.exp2(m_all - m_new)
                extra_w = jnp.exp2(es - m_new)
                l_all = l_all * rescale + extra_w
                m_all = m_new
            per_chunk.append((jnp.concatenate(pcs, axis=1), m_all, l_all, rescale, extra_w))

        vcat = jnp.concatenate([vp, vc], axis=0)
        v_bd = jnp.concatenate([vcat * head_mask[h] for h in range(HEADS_PER_CHUNK)], axis=0)
        res = []
        for (p_all, m_all, l_all, rescale, extra_w) in per_chunk:
            acc = jnp.dot(p_all, v_bd, preferred_element_type=_f32)
            if rescale is not None:
                acc = acc * rescale
            if far_key:
                acc = acc + extra_w * vp.astype(_f32)
            res.append((acc, m_all, l_all))
        outs.append(res)
    return outs


def _attn_pipeline(groups, *, head_ones=None):
    consts = _attn_consts()
    pending = None
    for grp in list(groups) + [None]:
        nxt = None
        if grp is not None:
            units = grp[0]()
            nxt = (grp, units, _attn_scores(units, consts, head_ones))
        if pending is not None:
            (_, sinks, consume), units_p, staged = pending
            consume(_attn_finish(units_p, staged, consts, head_ones, sinks))
        pending = nxt


def _merge(old, new):
    acc_o, m_o, l_o = old
    acc_n, m_n, l_n = new
    m = jnp.maximum(m_o, m_n)
    w_o = jnp.exp2(m_o - m)
    w_n = jnp.exp2(m_n - m)
    return w_o * acc_o + w_n * acc_n, m, w_o * l_o + w_n * l_n


def _attn_a_kernel(e_ref, q1, k1p, k1c, v1p, v1c, q4, k4p, k4c, v4p, v4c, q16, k16p, k16c, v16p, v16c,
                   o_ref, acc_ref, m_ref, l_ref, *, d4, d16):
    t = ATT_BLOCK
    has_prev = pl.program_id(1) > 0
    group16, group = 2, 4
    blocks4 = d16 // d4
    blocks1 = d16
    state = (acc_ref, m_ref, l_ref)

    def slab(x, s):
        return x[:, s * LANES:(s + 1) * LANES]

    def load_state(rows):
        return tuple(jnp.concatenate([ref[s, rows, :] for s in range(N_SLABS)], axis=1) for ref in state)

    def store_state(rows, vals):
        for ref, val in zip(state, vals):
            for s in range(N_SLABS):
                ref[s, rows, :] = slab(val, s)

    head_ones = e_ref[...]

    def body16(rg, carry):
        rs = [group16 * rg + u for u in range(group16)]

        def make():
            return [(q16[0, r], k16p[0, r], k16c[0, r], v16p[0, r], v16c[0, r], has_prev) for r in rs]

        def consume(outs):
            for r, res in zip(rs, outs):
                store_state(pl.ds(r, t, stride=d16), res[0])

        _attn_pipeline([(make, None, consume)], head_ones=head_ones)
        return carry

    lax.fori_loop(0, d16 // group16, body16, 0)

    def stream_units(base, q, kp0, kc, vp0, vc):
        units = []
        for bl in range(group):
            start = pl.multiple_of(base + bl * t, t)
            if bl == 0:
                before = pl.multiple_of(jnp.maximum(base - t, 0), t)
                first = base == 0
                kp = jnp.where(first, kp0, kc[pl.ds(before, t), :])
                vp = jnp.where(first, vp0, vc[pl.ds(before, t), :])
                ok = jnp.logical_or(has_prev, base > 0)
            else:
                before = pl.multiple_of(base + (bl - 1) * t, t)
                kp, vp, ok = kc[pl.ds(before, t), :], vc[pl.ds(before, t), :], True
            units.append((q[pl.ds(start, t), :], kp, kc[pl.ds(start, t), :], vp, vc[pl.ds(start, t), :], ok))
        return units

    groups4 = blocks4 // group

    def body4(i, carry):
        r = i // groups4
        base = pl.multiple_of((i % groups4) * (group * t), group * t)

        def make():
            return stream_units(base, q4.at[0, r], k4p[0, r], k4c.at[0, r], v4p[0, r], v4c.at[0, r])

        def consume(outs):
            for bl, res in enumerate(outs):
                rows = pl.ds((base + bl * t) * d4 + r, t, stride=d4)
                store_state(rows, _merge(load_state(rows), res[0]))

        _attn_pipeline([(make, None, consume)], head_ones=head_ones)
        return carry

    lax.fori_loop(0, d4 * groups4, body4, 0)

    def body1(g, carry):
        base = pl.multiple_of(g * (group * t), group * t)

        def make():
            return stream_units(base, q1.at[0], k1p[0], k1c.at[0], v1p[0], v1c.at[0])

        def consume(outs):
            for bl, res in enumerate(outs):
                rows = pl.ds(pl.multiple_of(base + bl * t, t), t)
                acc, _, l = _merge(load_state(rows), res[0])
                o_ref[0, rows, :] = (acc * (1.0 / l)).astype(o_ref.dtype)

        _attn_pipeline([(make, None, consume)], head_ones=head_ones)
        return carry

    lax.fori_loop(0, blocks1 // group, body1, 0)


def _attn_a(a1, a4, a16, head_ones):
    b, s, _ = a1.shape
    t = ATT_BLOCK
    (w1, d1), (w4, d4), (w16, d16) = DILATED_PAIRS
    assert d1 == 1 and w1 == t and w4 == t * d4 and w16 == t * d16 and d16 % d4 == 0
    sup = t * d16
    assert s % sup == 0
    n_chunks = A_WIDTH // HEAD_CHUNK
    qc, kc, vc = 0, n_chunks, 2 * n_chunks

    def nat(rows, per_sup, off, prev):
        def imap(bb, n, c):
            blk = jnp.maximum(n * per_sup - 1, 0) if prev else n
            return (bb, blk, off + c)
        return pl.BlockSpec((1, rows, HEAD_CHUNK), imap)

    def strm(d, rows, per_sup, off, prev):
        def imap(bb, n, c):
            blk = jnp.maximum(n * per_sup - 1, 0) if prev else n
            return (bb, 0, blk, off + c)
        return pl.BlockSpec((1, d, rows, HEAD_CHUNK), imap)

    in_specs = [_resident((HEAD_CHUNK, HEAD_CHUNK), lambda bb, n, c: (0, 0)),
                nat(sup, 1, qc, False),
                nat(t, sup // t, kc, True), nat(sup, 1, kc, False),
                nat(t, sup // t, vc, True), nat(sup, 1, vc, False),
                strm(d4, sup // d4, 1, qc, False),
                strm(d4, t, sup // d4 // t, kc, True), strm(d4, sup // d4, 1, kc, False),
                strm(d4, t, sup // d4 // t, vc, True), strm(d4, sup // d4, 1, vc, False),
                strm(d16, t, 1, qc, False),
                strm(d16, t, 1, kc, True), strm(d16, t, 1, kc, False),
                strm(d16, t, 1, vc, True), strm(d16, t, 1, vc, False)]
    state = pltpu.VMEM((N_SLABS, sup, LANES), _f32)
    return pl.pallas_call(
        functools.partial(_attn_a_kernel, d4=d4, d16=d16),
        out_shape=jax.ShapeDtypeStruct((b, s, A_WIDTH), _bf16),
        grid=(b, s // sup, n_chunks),
        in_specs=in_specs,
        out_specs=pl.BlockSpec((1, sup, HEAD_CHUNK), lambda bb, n, c: (bb, n, c)),
        scratch_shapes=[state, state, state],
        compiler_params=pltpu.CompilerParams(
            dimension_semantics=("parallel", "parallel", "parallel"), vmem_limit_bytes=V7X_VMEM_LIMIT),
        name="attn_a",
    )(head_ones, a1, a1, a1, a1, a1, a4, a4, a4, a4, a4, a16, a16, a16, a16, a16).reshape(b * s, A_WIDTH)


def _out_proj_kernel(oa_ref, ob_ref, q_ref, kvp_ref, kvc_ref, sink_ref, x_ref, w_ref, *rest_refs,
                     tm, tiles_per_seq, n_casts):
    cast_in, xo_ref, cast_out, oc_ref = (rest_refs[:n_casts], rest_refs[n_casts],
                                         rest_refs[n_casts + 1:2 * n_casts + 1], rest_refs[-1])
    for src, dst in zip(cast_in, cast_out):
        dst[...] = src[0].astype(dst.dtype)

    t = ATT_BLOCK
    d = w_ref.shape[1]
    has_prev = pl.program_id(0) % tiles_per_seq > 0
    blocks = tm // t
    group_width = C_WIDTH // C_KV_HEADS
    n = group_width // HEAD_CHUNK
    ab_width = A_WIDTH + B_WIDTH
    col_chunks = d // HEAD_CHUNK
    units_per_group = 2
    att_groups = C_KV_HEADS * blocks // units_per_group
    ab_parts = {}

    def project_ab(c):
        cols = slice(c * HEAD_CHUNK, (c + 1) * HEAD_CHUNK)
        ab_parts[c] = (jnp.dot(oa_ref[...], w_ref[0:A_WIDTH, cols], preferred_element_type=_f32) +
                       jnp.dot(ob_ref[...], w_ref[A_WIDTH:ab_width, cols], preferred_element_type=_f32))

    groups = []
    for gi in range(att_groups):
        g, b0 = divmod(gi * units_per_group, blocks)

        def make(gi=gi, g=g, b0=b0):
            for c in range(gi * col_chunks // att_groups, (gi + 1) * col_chunks // att_groups):
                project_ab(c)
            k_cols = slice(g * HEAD_CHUNK, (g + 1) * HEAD_CHUNK)
            v_cols = slice((C_KV_HEADS + g) * HEAD_CHUNK, (C_KV_HEADS + g + 1) * HEAD_CHUNK)
            q_cols = slice(g * group_width, (g + 1) * group_width)
            units = []
            for blk in range(b0, b0 + units_per_group):
                cur = slice(blk * t, (blk + 1) * t)
                if blk == 0:
                    kp, vp, ok = kvp_ref[:, k_cols], kvp_ref[:, v_cols], has_prev
                else:
                    before = slice((blk - 1) * t, blk * t)
                    kp, vp, ok = kvc_ref[before, k_cols], kvc_ref[before, v_cols], True
                units.append((q_ref[cur, q_cols], kp, kvc_ref[cur, k_cols], vp, kvc_ref[cur, v_cols], ok))
            return units

        def consume(outs, g=g, b0=b0):
            for blk, res in zip(range(b0, b0 + units_per_group), outs):
                for c, (acc, _, l) in enumerate(res):
                    col = g * group_width + c * HEAD_CHUNK
                    oc_ref[blk * t:(blk + 1) * t, col:col + HEAD_CHUNK] = (acc * (1.0 / l)).astype(oc_ref.dtype)

        sinks = [[sink_ref[g * n + c] for c in range(n)]] * units_per_group
        groups.append((make, sinks, consume))
    _attn_pipeline(groups)

    for c in range(col_chunks):
        cols = slice(c * HEAD_CHUNK, (c + 1) * HEAD_CHUNK)
        acc = ab_parts[c] + jnp.dot(oc_ref[...], w_ref[ab_width:, cols], preferred_element_type=_f32)
        xo_ref[:, cols] = x_ref[:, cols] + acc


def _out_proj(oa, ob, rest, kv_rep, sink_lanes, x, w, casts, *, batch, tm):
    m, d = x.shape
    steps = m // tm
    t = ATT_BLOCK
    assert m % tm == 0 and all(cw.shape[1] % (steps * BF16_ROWS) == 0 for cw, _ in casts)
    assert (m // batch) % tm == 0 and tm % (2 * t) == 0 and C_WINDOW == t and REST_WIDTH > C_WIDTH

    def cast_in(cw, layer):
        return pl.BlockSpec((1, cw.shape[1] // steps, cw.shape[2]), lambda i: (layer, i, 0))

    def cast_out(cw):
        return pl.BlockSpec((cw.shape[1] // steps, cw.shape[2]), lambda i: (i, 0))

    def rows(width):
        return pl.BlockSpec((tm, width), lambda i: (i, 0))

    kern = functools.partial(_out_proj_kernel, tm=tm, tiles_per_seq=m // batch // tm, n_casts=len(casts))
    return pl.pallas_call(
        kern,
        out_shape=[jax.ShapeDtypeStruct((m, d), _f32)] +
                  [jax.ShapeDtypeStruct(cw.shape[1:], _bf16) for cw, _ in casts],
        grid=(steps,),
        in_specs=[rows(A_WIDTH), rows(B_WIDTH), rows(C_WIDTH),
                  pl.BlockSpec((t, KV_REP_WIDTH), lambda i: (jnp.maximum(i * (tm // t) - 1, 0), 0)),
                  rows(KV_REP_WIDTH),
                  _resident(sink_lanes.shape, lambda i: (0, 0, 0)),
                  rows(d),
                  _resident(w.shape, lambda i: (0, 0))] + [cast_in(cw, layer) for cw, layer in casts],
        out_specs=[rows(d)] + [cast_out(cw) for cw, _ in casts],
        scratch_shapes=[pltpu.VMEM((tm, C_WIDTH), _bf16)],
        compiler_params=pltpu.CompilerParams(
            dimension_semantics=("parallel",), vmem_limit_bytes=V7X_VMEM_LIMIT),
        name="out_proj",
    )(oa, ob, rest, kv_rep, kv_rep, sink_lanes, x, w, *(cw for cw, _ in casts))


def _ffn_kernel(x_ref, g_ref, wg_ref, wu_ref, wd_ref, o_ref, h_ref):
    f = pl.program_id(1)

    @pl.when(f == 0)
    def _():
        x = x_ref[...]
        ms = jnp.mean(x * x, axis=-1, keepdims=True)
        h_ref[...] = (x * lax.rsqrt(ms + EPS) * g_ref[...]).astype(_bf16)
        o_ref[...] = x

    h = h_ref[...]
    gate = jnp.dot(h, wg_ref[...], preferred_element_type=_f32)
    up = jnp.dot(h, wu_ref[...], preferred_element_type=_f32)
    act = (gate * _sigmoid(gate) * up).astype(_bf16)
    o_ref[...] += jnp.dot(act, wd_ref[...], preferred_element_type=_f32)


def _ffn(x, g, wg, wu, wd, *, tm, tf):
    m, d = x.shape
    ff = wg.shape[1]
    assert m % tm == 0 and ff % tf == 0
    return pl.pallas_call(
        _ffn_kernel,
        out_shape=jax.ShapeDtypeStruct((m, d), _f32),
        grid=(m // tm, ff // tf),
        in_specs=[
            pl.BlockSpec((tm, d), lambda i, f: (i, 0)),
            pl.BlockSpec((1, d), lambda i, f: (0, 0)),
            pl.BlockSpec((d, tf), lambda i, f: (0, f)),
            pl.BlockSpec((d, tf), lambda i, f: (0, f)),
            pl.BlockSpec((tf, d), lambda i, f: (f, 0)),
        ],
        out_specs=pl.BlockSpec((tm, d), lambda i, f: (i, 0)),
        scratch_shapes=[pltpu.VMEM((tm, d), _bf16)],
        compiler_params=pltpu.CompilerParams(
            dimension_semantics=("parallel", "arbitrary"), vmem_limit_bytes=V7X_VMEM_LIMIT),
        name="swiglu_ffn",
    )(x, g, wg, wu, wd)


def _tile(m, pref):
    t = pref
    while m % t:
        t //= 2
    return t


def _qk_gain_and_flag(a_q_g, a_k_g, c_q_g, c_k_g):
    scale = HEAD_DIM ** -0.5 * LOG2E
    ones = lambda n: jnp.ones((n,), _f32)
    zeros = lambda n: jnp.zeros((n,), _f32)
    gain = jnp.concatenate([
        jnp.tile(a_q_g * scale, A_HEADS), jnp.tile(a_k_g, A_HEADS), ones(A_WIDTH + 2 * B_WIDTH),
        jnp.tile(c_q_g * scale, C_Q_HEADS), jnp.tile(c_k_g, C_KV_HEADS), ones(C_KV_WIDTH)])
    flag = jnp.concatenate([ones(2 * A_WIDTH), zeros(A_WIDTH + 2 * B_WIDTH),
                            ones(C_WIDTH + C_KV_WIDTH), zeros(C_KV_WIDTH)])
    return gain.reshape(1, IN_WIDTH), flag.reshape(1, IN_WIDTH)


def kernel(x, norm1_g, w_in, a_q_g, a_k_g, conv_w, conv_b, conv_ln_g, conv_ln_b,
           c_q_g, c_k_g, c_sinks, w_out, norm2_g, w_gate, w_up, w_down):
    b, s, d = x.shape
    m = b * s
    depth = w_in.shape[0]
    lane = jnp.arange(HEAD_CHUNK) // HEAD_DIM
    head_ones = (lane[:, None] == lane[None, :]).astype(_bf16)
    dst = jnp.arange(KV_REP_WIDTH)
    kv_rep = (jnp.arange(HEAD_CHUNK)[:, None] ==
              (dst // HEAD_CHUNK) * HEAD_DIM + dst % HEAD_DIM).astype(_bf16)
    w_in_l = w_in[0].astype(_bf16)
    xf = x.reshape(m, d)
    for l in range(depth):
        gain, flag = _qk_gain_and_flag(a_q_g[l], a_k_g[l], c_q_g[l], c_k_g[l])
        a1, a4, a16, ob, rest, kvr, w_out_l = _in_proj(
            xf, norm1_g[l].reshape(1, d), w_in_l, head_ones, kv_rep, gain, flag,
            conv_w[l], conv_b[l].reshape(1, -1), conv_ln_g[l].reshape(1, -1), conv_ln_b[l].reshape(1, -1),
            w_out, l, batch=b, tm=_tile(s, 512))
        oa = _attn_a(a1.reshape(b, s, A_QKV_WIDTH), a4, a16, head_ones)
        sink_lanes = jnp.repeat(c_sinks[l] * LOG2E, HEAD_DIM).reshape(C_WIDTH // HEAD_CHUNK, 1, HEAD_CHUNK)
        casts = [(w_gate, l), (w_up, l), (w_down, l)] + ([(w_in, l + 1)] if l + 1 < depth else [])
        xf, wg, wu, wd, *nxt = _out_proj(oa, ob, rest, kvr, sink_lanes, xf, w_out_l, casts,
                                         batch=b, tm=_tile(s, 512))
        if nxt:
            w_in_l = nxt[0]
        xf = _ffn(xf, norm2_g[l].reshape(1, d), wg, wu, wd, tm=_tile(m, 1024), tf=512)
    return xf.reshape(b, s, d)
```

```python
import functools
import math

import jax
import jax.numpy as jnp
from jax import lax
from jax.experimental import pallas as pl
from jax.experimental.pallas import tpu as pltpu

HEAD_DIM = 64
A_HEADS = 8
A_WIDTH = A_HEADS * HEAD_DIM
DILATED_PAIRS = ((128, 1), (512, 4), (2048, 16))
B_WIDTH = 512
CONV_WIDTH = 31
C_Q_HEADS = 16
C_KV_HEADS = 2
C_WIDTH = C_Q_HEADS * HEAD_DIM
C_KV_WIDTH = C_KV_HEADS * HEAD_DIM
C_WINDOW = 128
IN_WIDTH = 3 * A_WIDTH + 2 * B_WIDTH + C_WIDTH + 2 * C_KV_WIDTH
EPS = 1e-6
LOG2E = math.log2(math.e)

LANES = 128
SUBLANES = 8
BF16_ROWS = 16
ATT_BLOCK = 128
HEAD_CHUNK = 256
HEADS_PER_CHUNK = HEAD_CHUNK // HEAD_DIM
N_SLABS = HEAD_CHUNK // LANES
A_QKV_WIDTH = 3 * A_WIDTH
REST_WIDTH = C_WIDTH + 2 * C_KV_WIDTH
KV_REP_WIDTH = 2 * C_KV_HEADS * HEAD_CHUNK
CONV_HALO = 32
MASKED_SCORE = -1e30
V7X_VMEM_LIMIT = 56 * 1024 * 1024

_bf16 = jnp.bfloat16
_f32 = jnp.float32


def _sigmoid(v):
    return 1.0 / (1.0 + jnp.exp(-v))


def _resident(shape, index_map):
    return pl.BlockSpec(shape, index_map, pipeline_mode=pl.Buffered(1))


CONV_ROWS = 64


def _conv_tasks(h_halves, is_first, w_ref, b_ref, lg_ref, lb_ref, ob_ref, hext_ref, shift_ref, y_ref, *, tm):
    col_groups = [slice(cg * LANES, (cg + 1) * LANES) for cg in range(B_WIDTH // LANES)]
    shift_rows = tm + CONV_HALO - SUBLANES
    first = CONV_HALO - (CONV_WIDTH - 1)

    def fill(pace):
        tail = hext_ref[tm:tm + CONV_HALO, :]
        hext_ref[0:CONV_HALO, :] = jnp.where(is_first, 0.0, tail)
        for half, h in enumerate(h_halves):
            hext_ref[CONV_HALO:CONV_HALO + tm, half * HEAD_CHUNK:(half + 1) * HEAD_CHUNK] = h

    def shift(j, pace):
        for cs in col_groups:
            shift_ref[j - 1, :, cs] = hext_ref[j:j + shift_rows, cs]

    def taps(cs, r0, pace):
        acc = jnp.zeros((CONV_ROWS, LANES), _f32) + pace
        for k in range(CONV_WIDTH):
            a, j = divmod(first + k, SUBLANES)
            rows = slice(r0 + a * SUBLANES, r0 + a * SUBLANES + CONV_ROWS)
            tap = hext_ref[rows, cs] if j == 0 else shift_ref[j - 1, rows, cs]
            acc = acc + w_ref[k:k + 1, cs] * tap
        y_ref[r0:r0 + CONV_ROWS, cs] = acc + b_ref[:, cs]

    def finish(r0, pace):
        y = y_ref[r0:r0 + CONV_ROWS, :]
        mu = jnp.mean(y, axis=-1, keepdims=True)
        yc = y - mu
        var = jnp.mean(yc * yc, axis=-1, keepdims=True)
        z = yc * lax.rsqrt(var + EPS) * lg_ref[...] + lb_ref[...]
        ob_ref[r0:r0 + CONV_ROWS, :] = (z * _sigmoid(z)).astype(ob_ref.dtype)

    tasks = [fill] + [functools.partial(shift, j) for j in range(1, SUBLANES)]
    for r0 in range(0, tm, CONV_ROWS):
        tasks += [functools.partial(taps, cs, r0) for cs in col_groups]
        tasks.append(functools.partial(finish, r0))
    return tasks


def _in_proj_kernel(x_ref, g_ref, w_ref, e_ref, rep_ref, gain_ref, flag_ref, cw_ref, cb_ref, lg_ref, lb_ref,
                    wo_ref, a1_ref, a4_ref, a16_ref, ob_ref, rest_ref, kv_ref, wo_out,
                    h_ref, slab_ref, slab4_ref, hext_ref, shift_ref, y_ref, *, tm, tiles_per_seq):
    wo_out[...] = wo_ref[0].astype(wo_out.dtype)
    x = x_ref[...]
    ms = jnp.mean(x * x, axis=-1, keepdims=True)
    h_ref[...] = (x * lax.rsqrt(ms + EPS) * g_ref[...]).astype(_bf16)

    n_chunks = IN_WIDTH // HEAD_CHUNK
    a_chunks = A_QKV_WIDTH // HEAD_CHUNK
    qk_chunks = 2 * A_WIDTH // HEAD_CHUNK
    conv_chunks = 2 * B_WIDTH // HEAD_CHUNK
    cq_chunk = a_chunks + conv_chunks
    order = list(range(a_chunks, cq_chunk)) + list(range(a_chunks)) + list(range(cq_chunk, n_chunks))

    def project(chunk):
        return jnp.dot(h_ref[...], w_ref[:, chunk * HEAD_CHUNK:(chunk + 1) * HEAD_CHUNK],
                       preferred_element_type=_f32)

    conv_in, conv_todo = [], []
    a_next = project(order[0])
    for pos, chunk in enumerate(order):
        sl = slice(chunk * HEAD_CHUNK, (chunk + 1) * HEAD_CHUNK)
        a = a_next
        if pos + 1 < n_chunks:
            a_next = project(order[pos + 1])
        share = -(-len(conv_todo) // max(n_chunks - 1 - pos, 1))
        if share:
            bits = pltpu.bitcast(a[0:SUBLANES, 0:LANES], jnp.int32)
            pace = lax.shift_right_logical(lax.shift_right_logical(bits, 16), 16)[0:1, :].astype(_f32)
            for task in conv_todo[:share]:
                task(pace)
            del conv_todo[:share]
        if chunk < qk_chunks or chunk >= cq_chunk:
            ss = jnp.dot((a * a).astype(_bf16), e_ref[...], preferred_element_type=_f32)
            yn = a * lax.rsqrt(ss * (1.0 / HEAD_DIM) + EPS) * gain_ref[:, sl]
            a = jnp.where(flag_ref[:, sl] > 0.0, yn, a) if chunk == n_chunks - 1 else yn
        if chunk < a_chunks:
            a1_ref[:, sl] = a.astype(_bf16)
            base = (chunk % 2) * N_SLABS
            for s in range(N_SLABS):
                slab_ref[base + s] = a[:, s * LANES:(s + 1) * LANES]
            d4, d16 = DILATED_PAIRS[1][1], DILATED_PAIRS[2][1]
            for s in range(N_SLABS):
                cs = slice(chunk * HEAD_CHUNK + s * LANES, chunk * HEAD_CHUNK + (s + 1) * LANES)
                for r in range(d4):
                    stream = slab_ref[base + s, pl.ds(r, tm // d4, stride=d4), :]
                    a4_ref[0, r, :, cs] = stream.astype(_bf16)
                    slab4_ref[base + s, r] = stream
                for r in range(d16):
                    a16_ref[0, r, :, cs] = slab4_ref[base + s, r % d4,
                                                     pl.ds(r // d4, tm // d16, stride=d16 // d4), :].astype(_bf16)
        elif chunk < cq_chunk:
            conv_in.append(a)
            if len(conv_in) == conv_chunks:
                half = conv_chunks // 2
                glu = [conv_in[c] * _sigmoid(conv_in[half + c]) for c in range(half)]
                conv_todo = _conv_tasks(glu, pl.program_id(0) % tiles_per_seq == 0, cw_ref, cb_ref, lg_ref,
                                        lb_ref, ob_ref, hext_ref, shift_ref, y_ref, tm=tm)
        else:
            rsl = slice(sl.start - cq_chunk * HEAD_CHUNK, sl.stop - cq_chunk * HEAD_CHUNK)
            ab = a.astype(_bf16)
            rest_ref[:, rsl] = ab
            if chunk == n_chunks - 1:
                kv_ref[...] = jnp.dot(ab, rep_ref[...], preferred_element_type=_f32).astype(_bf16)


def _in_proj(x, g, w, head_ones, kv_rep, gain, flag, conv_w, conv_b, ln_g, ln_b, w_out, layer, *, batch, tm):
    m, d = x.shape
    s = m // batch
    n = w.shape[1]
    steps = m // tm
    assert n == IN_WIDTH and s % tm == 0 and (IN_WIDTH - 2 * C_KV_WIDTH) % HEAD_CHUNK == 0
    assert w_out.shape[1] % (steps * BF16_ROWS) == 0
    wo_rows = w_out.shape[1] // steps
    assert tm % CONV_ROWS == 0 and CONV_HALO >= CONV_WIDTH - 1 and CONV_HALO % SUBLANES == 0
    tiles_per_seq = s // tm

    def stream_spec(dil):
        return pl.BlockSpec((1, dil, tm // dil, A_QKV_WIDTH),
                            lambda i: (i // tiles_per_seq, 0, i % tiles_per_seq, 0))

    def rows(width):
        return pl.BlockSpec((tm, width), lambda i: (i, 0))

    def whole(arr):
        return _resident(arr.shape, lambda i: (0,) * arr.ndim)

    d4, d16 = DILATED_PAIRS[1][1], DILATED_PAIRS[2][1]
    return pl.pallas_call(
        functools.partial(_in_proj_kernel, tm=tm, tiles_per_seq=tiles_per_seq),
        out_shape=(jax.ShapeDtypeStruct((m, A_QKV_WIDTH), _bf16),
                   jax.ShapeDtypeStruct((batch, d4, s // d4, A_QKV_WIDTH), _bf16),
                   jax.ShapeDtypeStruct((batch, d16, s // d16, A_QKV_WIDTH), _bf16),
                   jax.ShapeDtypeStruct((m, B_WIDTH), _bf16),
                   jax.ShapeDtypeStruct((m, REST_WIDTH), _bf16),
                   jax.ShapeDtypeStruct((m, KV_REP_WIDTH), _bf16),
                   jax.ShapeDtypeStruct(w_out.shape[1:], _bf16)),
        grid=(steps,),
        in_specs=[
            rows(d),
            whole(g),
            whole(w),
            whole(head_ones), whole(kv_rep), whole(gain), whole(flag),
            whole(conv_w), whole(conv_b), whole(ln_g), whole(ln_b),
            pl.BlockSpec((1, wo_rows, w_out.shape[2]), lambda i: (layer, i, 0)),
        ],
        out_specs=(rows(A_QKV_WIDTH), stream_spec(d4), stream_spec(d16), rows(B_WIDTH), rows(REST_WIDTH),
                   rows(KV_REP_WIDTH), pl.BlockSpec((wo_rows, w_out.shape[2]), lambda i: (i, 0))),
        scratch_shapes=[pltpu.VMEM((tm, d), _bf16), pltpu.VMEM((2 * N_SLABS, tm, LANES), _f32),
                        pltpu.VMEM((2 * N_SLABS, d4, tm // d4, LANES), _f32),
                        pltpu.VMEM((CONV_HALO + tm, B_WIDTH), _f32),
                        pltpu.VMEM((SUBLANES - 1, tm + CONV_HALO - SUBLANES, B_WIDTH), _f32),
                        pltpu.VMEM((tm, B_WIDTH), _f32)],
        compiler_params=pltpu.CompilerParams(
            dimension_semantics=("arbitrary",), vmem_limit_bytes=V7X_VMEM_LIMIT),
        name="in_proj",
    )(x, g, w, head_ones, kv_rep, gain, flag, conv_w, conv_b, ln_g, ln_b, w_out)


def _per_head_lanes(cols):
    half = lax.broadcasted_iota(jnp.int32, (1, LANES), 1) < HEAD_DIM
    return jnp.concatenate([jnp.where(half, cols[0], cols[1]), jnp.where(half, cols[2], cols[3])], axis=1)


def _attn_consts():
    t = ATT_BLOCK
    lane_head = lax.broadcasted_iota(jnp.int32, (1, HEAD_CHUNK), 1) // HEAD_DIM
    head_mask = [(lane_head == h).astype(_bf16) for h in range(HEADS_PER_CHUNK)]
    lower = lax.broadcasted_iota(jnp.int32, (t, t), 1) <= lax.broadcasted_iota(jnp.int32, (t, t), 0)
    return head_mask, lower, lower.astype(_bf16)


def _attn_scores(units, consts, head_ones):
    head_mask = consts[0]
    scores, far_scores = [], []
    for (q, kp, kc, vp, vc, has_prev) in units:
        n = q.shape[1] // HEAD_CHUNK
        qs = jnp.concatenate([q[:, c * HEAD_CHUNK:(c + 1) * HEAD_CHUNK] * head_mask[h]
                              for c in range(n) for h in range(HEADS_PER_CHUNK)], axis=0)
        kcat = jnp.concatenate([kp, kc], axis=0)
        scores.append(lax.dot_general(qs, kcat, (((1,), (1,)), ((), ())), preferred_element_type=_f32))
    if head_ones is not None:
        for (q, kp, kc, vp, vc, has_prev) in units:
            far_scores.append(jnp.dot(q * kp, head_ones, preferred_element_type=_f32))
    return scores, far_scores


def _attn_finish(units, staged, consts, head_ones, sinks):
    t = ATT_BLOCK
    head_mask, lower, lower_bf = consts
    scores, far_scores = staged
    far_key = head_ones is not None
    outs = []
    for u, (q, kp, kc, vp, vc, has_prev) in enumerate(units):
        per_chunk = []
        for c in range(q.shape[1] // HEAD_CHUNK):
            ms, ls, pcs = [], [], []
            for h in range(HEADS_PER_CHUNK):
                base = (c * HEADS_PER_CHUNK + h) * t
                s2 = scores[u][base:base + t]
                s_prev = s2[:, :t]
                if has_prev is not True:
                    s_prev = jnp.where(has_prev, s_prev, MASKED_SCORE)
                s = jnp.where(lower, s2[:, t:], s_prev)
                m = jnp.max(s, axis=-1, keepdims=True)
                p = jnp.exp2(s - m)
                ls.append(jnp.sum(p, axis=-1, keepdims=True))
                ms.append(m)
                pb = p.astype(_bf16)
                p_cur = pb * lower_bf
                pcs.append(pb - p_cur)
                pcs.append(p_cur)
            m_all = _per_head_lanes(ms)
            l_all = _per_head_lanes(ls)
            rescale = extra_w = None
            if far_key or sinks is not None:
                if far_key:
                    es = far_scores[u]
                    if has_prev is not True:
                        es = jnp.where(has_prev, es, MASKED_SCORE)
                else:
                    es = sinks[u][c]
                m_new = jnp.maximum(m_all, es)
                rescale = jnp.exp2(m_all - m_new)
                extra_w = jnp.exp2(es - m_new)
                l_all = l_all * rescale + extra_w
                m_all = m_new
            per_chunk.append((jnp.concatenate(pcs, axis=1), m_all, l_all, rescale, extra_w))

        vcat = jnp.concatenate([vp, vc], axis=0)
        v_bd = jnp.concatenate([vcat * head_mask[h] for h in range(HEADS_PER_CHUNK)], axis=0)
        res = []
        for (p_all, m_all, l_all, rescale, extra_w) in per_chunk:
            acc = jnp.dot(p_all, v_bd, preferred_element_type=_f32)
            if rescale is not None:
                acc = acc * rescale
            if far_key:
                acc = acc + extra_w * vp.astype(_f32)
            res.append((acc, m_all, l_all))
        outs.append(res)
    return outs


def _attn_pipeline(groups, *, head_ones=None):
    consts = _attn_consts()
    pending = None
    for grp in list(groups) + [None]:
        nxt = None
        if grp is not None:
            units = grp[0]()
            nxt = (grp, units, _attn_scores(units, consts, head_ones))
        if pending is not None:
            (_, sinks, consume), units_p, staged = pending
            consume(_attn_finish(units_p, staged, consts, head_ones, sinks))
        pending = nxt


def _merge(old, new):
    acc_o, m_o, l_o = old
    acc_n, m_n, l_n = new
    m = jnp.maximum(m_o, m_n)
    w_o = jnp.exp2(m_o - m)
    w_n = jnp.exp2(m_n - m)
    return w_o * acc_o + w_n * acc_n, m, w_o * l_o + w_n * l_n


def _attn_a_kernel(e_ref, q1, k1p, k1c, v1p, v1c, q4, k4p, k4c, v4p, v4c, q16, k16p, k16c, v16p, v16c,
                   o_ref, acc_ref, m_ref, l_ref, *, d4, d16):
    t = ATT_BLOCK
    has_prev = pl.program_id(1) > 0
    group16, group = 2, 4
    blocks4 = d16 // d4
    blocks1 = d16
    state = (acc_ref, m_ref, l_ref)

    def slab(x, s):
        return x[:, s * LANES:(s + 1) * LANES]

    def load_state(rows, refs=state):
        return tuple(jnp.concatenate([ref[s, rows, :] for s in range(N_SLABS)], axis=1) for ref in refs)

    def store_state(rows, vals, refs=state):
        for ref, val in zip(refs, vals):
            for s in range(N_SLABS):
                ref[s, rows, :] = slab(val, s)

    head_ones = e_ref[...]

    def body16(rg, carry):
        rs = [group16 * rg + u for u in range(group16)]

        def make():
            return [(q16[0, r], k16p[0, r], k16c[0, r], v16p[0, r], v16c[0, r], has_prev) for r in rs]

        def consume(outs):
            for r, res in zip(rs, outs):
                acc, m, l = res[0]
                store_state(pl.ds(r, t, stride=d16), (acc * (1.0 / l), m + jnp.log2(l)), state[:2])

        _attn_pipeline([(make, None, consume)], head_ones=head_ones)
        return carry

    lax.fori_loop(0, d16 // group16, body16, 0)

    def stream_units(base, q, kp0, kc, vp0, vc):
        units = []
        for bl in range(group):
            start = pl.multiple_of(base + bl * t, t)
            if bl == 0:
                before = pl.multiple_of(jnp.maximum(base - t, 0), t)
                first = base == 0
                kp = jnp.where(first, kp0, kc[pl.ds(before, t), :])
                vp = jnp.where(first, vp0, vc[pl.ds(before, t), :])
                ok = jnp.logical_or(has_prev, base > 0)
            else:
                before = pl.multiple_of(base + (bl - 1) * t, t)
                kp, vp, ok = kc[pl.ds(before, t), :], vc[pl.ds(before, t), :], True
            units.append((q[pl.ds(start, t), :], kp, kc[pl.ds(start, t), :], vp, vc[pl.ds(start, t), :], ok))
        return units

    groups4 = blocks4 // group

    def body4(i, carry):
        r = i // groups4
        base = pl.multiple_of((i % groups4) * (group * t), group * t)

        units = stream_units(base, q4.at[0, r], k4p[0, r], k4c.at[0, r], v4p[0, r], v4c.at[0, r])

        def consume(bl, outs):
            rows = pl.ds((base + bl * t) * d4 + r, t, stride=d4)
            acc_o, m_o = load_state(rows, state[:2])
            store_state(rows, _merge((acc_o, m_o, jnp.ones_like(m_o)), outs[0][0]))

        _attn_pipeline([((lambda u=u: [units[u]]), None, functools.partial(consume, u)) for u in range(group)],
                       head_ones=head_ones)
        return carry

    lax.fori_loop(0, d4 * groups4, body4, 0)

    def body1(g, carry):
        base = pl.multiple_of(g * (group * t), group * t)

        units = stream_units(base, q1.at[0], k1p[0], k1c.at[0], v1p[0], v1c.at[0])

        def consume(bl, outs):
            rows = pl.ds(pl.multiple_of(base + bl * t, t), t)
            acc, _, l = _merge(load_state(rows), outs[0][0])
            o_ref[0, rows, :] = (acc * (1.0 / l)).astype(o_ref.dtype)

        _attn_pipeline([((lambda u=u: [units[u]]), None, functools.partial(consume, u)) for u in range(group)],
                       head_ones=head_ones)
        return carry

    lax.fori_loop(0, blocks1 // group, body1, 0)


def _attn_a(a1, a4, a16, head_ones):
    b, s, _ = a1.shape
    t = ATT_BLOCK
    (w1, d1), (w4, d4), (w16, d16) = DILATED_PAIRS
    assert d1 == 1 and w1 == t and w4 == t * d4 and w16 == t * d16 and d16 % d4 == 0
    sup = t * d16
    assert s % sup == 0
    n_chunks = A_WIDTH // HEAD_CHUNK
    qc, kc, vc = 0, n_chunks, 2 * n_chunks

    def nat(rows, per_sup, off, prev):
        def imap(bb, n, c):
            blk = jnp.maximum(n * per_sup - 1, 0) if prev else n
            return (bb, blk, off + c)
        return pl.BlockSpec((1, rows, HEAD_CHUNK), imap)

    def strm(d, rows, per_sup, off, prev):
        def imap(bb, n, c):
            blk = jnp.maximum(n * per_sup - 1, 0) if prev else n
            return (bb, 0, blk, off + c)
        return pl.BlockSpec((1, d, rows, HEAD_CHUNK), imap)

    in_specs = [_resident((HEAD_CHUNK, HEAD_CHUNK), lambda bb, n, c: (0, 0)),
                nat(sup, 1, qc, False),
                nat(t, sup // t, kc, True), nat(sup, 1, kc, False),
                nat(t, sup // t, vc, True), nat(sup, 1, vc, False),
                strm(d4, sup // d4, 1, qc, False),
                strm(d4, t, sup // d4 // t, kc, True), strm(d4, sup // d4, 1, kc, False),
                strm(d4, t, sup // d4 // t, vc, True), strm(d4, sup // d4, 1, vc, False),
                strm(d16, t, 1, qc, False),
                strm(d16, t, 1, kc, True), strm(d16, t, 1, kc, False),
                strm(d16, t, 1, vc, True), strm(d16, t, 1, vc, False)]
    state = pltpu.VMEM((N_SLABS, sup, LANES), _f32)
    return pl.pallas_call(
        functools.partial(_attn_a_kernel, d4=d4, d16=d16),
        out_shape=jax.ShapeDtypeStruct((b, s, A_WIDTH), _bf16),
        grid=(b, s // sup, n_chunks),
        in_specs=in_specs,
        out_specs=pl.BlockSpec((1, sup, HEAD_CHUNK), lambda bb, n, c: (bb, n, c)),
        scratch_shapes=[state, state, state],
        compiler_params=pltpu.CompilerParams(
            dimension_semantics=("parallel", "parallel", "parallel"), vmem_limit_bytes=V7X_VMEM_LIMIT),
        name="attn_a",
    )(head_ones, a1, a1, a1, a1, a1, a4, a4, a4, a4, a4, a16, a16, a16, a16, a16).reshape(b * s, A_WIDTH)


def _out_proj_kernel(oa_ref, ob_ref, q_ref, kvp_ref, kvc_ref, sink_ref, x_ref, w_ref, *rest_refs,
                     tm, tiles_per_seq, n_casts):
    cast_in, xo_ref, cast_out, oc_ref = (rest_refs[:n_casts], rest_refs[n_casts],
                                         rest_refs[n_casts + 1:2 * n_casts + 1], rest_refs[-1])
    for src, dst in zip(cast_in, cast_out):
        dst[...] = src[0].astype(dst.dtype)

    t = ATT_BLOCK
    d = w_ref.shape[1]
    has_prev = pl.program_id(0) % tiles_per_seq > 0
    blocks = tm // t
    group_width = C_WIDTH // C_KV_HEADS
    n = group_width // HEAD_CHUNK
    ab_width = A_WIDTH + B_WIDTH
    col_chunks = d // HEAD_CHUNK
    units_per_group = 2
    att_groups = C_KV_HEADS * blocks // units_per_group
    ab_parts = {}

    def project_ab(c):
        cols = slice(c * HEAD_CHUNK, (c + 1) * HEAD_CHUNK)
        ab_parts[c] = (jnp.dot(oa_ref[...], w_ref[0:A_WIDTH, cols], preferred_element_type=_f32) +
                       jnp.dot(ob_ref[...], w_ref[A_WIDTH:ab_width, cols], preferred_element_type=_f32))

    groups = []
    for gi in range(att_groups):
        g, b0 = divmod(gi * units_per_group, blocks)

        def make(gi=gi, g=g, b0=b0):
            for c in range(gi * col_chunks // att_groups, (gi + 1) * col_chunks // att_groups):
                project_ab(c)
            k_cols = slice(g * HEAD_CHUNK, (g + 1) * HEAD_CHUNK)
            v_cols = slice((C_KV_HEADS + g) * HEAD_CHUNK, (C_KV_HEADS + g + 1) * HEAD_CHUNK)
            q_cols = slice(g * group_width, (g + 1) * group_width)
            units = []
            for blk in range(b0, b0 + units_per_group):
                cur = slice(blk * t, (blk + 1) * t)
                if blk == 0:
                    kp, vp, ok = kvp_ref[:, k_cols], kvp_ref[:, v_cols], has_prev
                else:
                    before = slice((blk - 1) * t, blk * t)
                    kp, vp, ok = kvc_ref[before, k_cols], kvc_ref[before, v_cols], True
                units.append((q_ref[cur, q_cols], kp, kvc_ref[cur, k_cols], vp, kvc_ref[cur, v_cols], ok))
            return units

        def consume(outs, g=g, b0=b0):
            for blk, res in zip(range(b0, b0 + units_per_group), outs):
                for c, (acc, _, l) in enumerate(res):
                    col = g * group_width + c * HEAD_CHUNK
                    oc_ref[blk * t:(blk + 1) * t, col:col + HEAD_CHUNK] = (acc * (1.0 / l)).astype(oc_ref.dtype)

        sinks = [[sink_ref[g * n + c] for c in range(n)]] * units_per_group
        groups.append((make, sinks, consume))
    _attn_pipeline(groups)

    for c in range(col_chunks):
        cols = slice(c * HEAD_CHUNK, (c + 1) * HEAD_CHUNK)
        acc = ab_parts[c] + jnp.dot(oc_ref[...], w_ref[ab_width:, cols], preferred_element_type=_f32)
        xo_ref[:, cols] = x_ref[:, cols] + acc


def _out_proj(oa, ob, rest, kv_rep, sink_lanes, x, w, casts, *, batch, tm):
    m, d = x.shape
    steps = m // tm
    t = ATT_BLOCK
    assert m % tm == 0 and all(cw.shape[1] % (steps * BF16_ROWS) == 0 for cw, _ in casts)
    assert (m // batch) % tm == 0 and tm % (2 * t) == 0 and C_WINDOW == t and REST_WIDTH > C_WIDTH

    def cast_in(cw, layer):
        return pl.BlockSpec((1, cw.shape[1] // steps, cw.shape[2]), lambda i: (layer, i, 0))

    def cast_out(cw):
        return pl.BlockSpec((cw.shape[1] // steps, cw.shape[2]), lambda i: (i, 0))

    def rows(width):
        return pl.BlockSpec((tm, width), lambda i: (i, 0))

    kern = functools.partial(_out_proj_kernel, tm=tm, tiles_per_seq=m // batch // tm, n_casts=len(casts))
    return pl.pallas_call(
        kern,
        out_shape=[jax.ShapeDtypeStruct((m, d), _f32)] +
                  [jax.ShapeDtypeStruct(cw.shape[1:], _bf16) for cw, _ in casts],
        grid=(steps,),
        in_specs=[rows(A_WIDTH), rows(B_WIDTH), rows(C_WIDTH),
                  pl.BlockSpec((t, KV_REP_WIDTH), lambda i: (jnp.maximum(i * (tm // t) - 1, 0), 0)),
                  rows(KV_REP_WIDTH),
                  _resident(sink_lanes.shape, lambda i: (0, 0, 0)),
                  rows(d),
                  _resident(w.shape, lambda i: (0, 0))] + [cast_in(cw, layer) for cw, layer in casts],
        out_specs=[rows(d)] + [cast_out(cw) for cw, _ in casts],
        scratch_shapes=[pltpu.VMEM((tm, C_WIDTH), _bf16)],
        compiler_params=pltpu.CompilerParams(
            dimension_semantics=("parallel",), vmem_limit_bytes=V7X_VMEM_LIMIT),
        name="out_proj",
    )(oa, ob, rest, kv_rep, kv_rep, sink_lanes, x, w, *(cw for cw, _ in casts))


def _ffn_kernel(x_ref, g_ref, wg_ref, wu_ref, wd_ref, o_ref, h_ref):
    f = pl.program_id(1)

    @pl.when(f == 0)
    def _():
        x = x_ref[...]
        ms = jnp.mean(x * x, axis=-1, keepdims=True)
        h_ref[...] = (x * lax.rsqrt(ms + EPS) * g_ref[...]).astype(_bf16)
        o_ref[...] = x

    h = h_ref[...]
    gate = jnp.dot(h, wg_ref[...], preferred_element_type=_f32)
    up = jnp.dot(h, wu_ref[...], preferred_element_type=_f32)
    act = (gate * _sigmoid(gate) * up).astype(_bf16)
    o_ref[...] += jnp.dot(act, wd_ref[...], preferred_element_type=_f32)


def _ffn(x, g, wg, wu, wd, *, tm, tf):
    m, d = x.shape
    ff = wg.shape[1]
    assert m % tm == 0 and ff % tf == 0
    return pl.pallas_call(
        _ffn_kernel,
        out_shape=jax.ShapeDtypeStruct((m, d), _f32),
        grid=(m // tm, ff // tf),
        in_specs=[
            pl.BlockSpec((tm, d), lambda i, f: (i, 0)),
            pl.BlockSpec((1, d), lambda i, f: (0, 0)),
            pl.BlockSpec((d, tf), lambda i, f: (0, f)),
            pl.BlockSpec((d, tf), lambda i, f: (0, f)),
            pl.BlockSpec((tf, d), lambda i, f: (f, 0)),
        ],
        out_specs=pl.BlockSpec((tm, d), lambda i, f: (i, 0)),
        scratch_shapes=[pltpu.VMEM((tm, d), _bf16)],
        compiler_params=pltpu.CompilerParams(
            dimension_semantics=("parallel", "arbitrary"), vmem_limit_bytes=V7X_VMEM_LIMIT),
        name="swiglu_ffn",
    )(x, g, wg, wu, wd)


def _tile(m, pref):
    t = pref
    while m % t:
        t //= 2
    return t


def _qk_gain_and_flag(a_q_g, a_k_g, c_q_g, c_k_g):
    scale = HEAD_DIM ** -0.5 * LOG2E
    ones = lambda n: jnp.ones((n,), _f32)
    zeros = lambda n: jnp.zeros((n,), _f32)
    gain = jnp.concatenate([
        jnp.tile(a_q_g * scale, A_HEADS), jnp.tile(a_k_g, A_HEADS), ones(A_WIDTH + 2 * B_WIDTH),
        jnp.tile(c_q_g * scale, C_Q_HEADS), jnp.tile(c_k_g, C_KV_HEADS), ones(C_KV_WIDTH)])
    flag = jnp.concatenate([ones(2 * A_WIDTH), zeros(A_WIDTH + 2 * B_WIDTH),
                            ones(C_WIDTH + C_KV_WIDTH), zeros(C_KV_WIDTH)])
    return gain.reshape(1, IN_WIDTH), flag.reshape(1, IN_WIDTH)


def kernel(x, norm1_g, w_in, a_q_g, a_k_g, conv_w, conv_b, conv_ln_g, conv_ln_b,
           c_q_g, c_k_g, c_sinks, w_out, norm2_g, w_gate, w_up, w_down):
    b, s, d = x.shape
    m = b * s
    depth = w_in.shape[0]
    lane = jnp.arange(HEAD_CHUNK) // HEAD_DIM
    head_ones = (lane[:, None] == lane[None, :]).astype(_bf16)
    dst = jnp.arange(KV_REP_WIDTH)
    kv_rep = (jnp.arange(HEAD_CHUNK)[:, None] ==
              (dst // HEAD_CHUNK) * HEAD_DIM + dst % HEAD_DIM).astype(_bf16)
    w_in_l = w_in[0].astype(_bf16)
    xf = x.reshape(m, d)
    for l in range(depth):
        gain, flag = _qk_gain_and_flag(a_q_g[l], a_k_g[l], c_q_g[l], c_k_g[l])
        a1, a4, a16, ob, rest, kvr, w_out_l = _in_proj(
            xf, norm1_g[l].reshape(1, d), w_in_l, head_ones, kv_rep, gain, flag,
            conv_w[l], conv_b[l].reshape(1, -1), conv_ln_g[l].reshape(1, -1), conv_ln_b[l].reshape(1, -1),
            w_out, l, batch=b, tm=_tile(s, 512))
        oa = _attn_a(a1.reshape(b, s, A_QKV_WIDTH), a4, a16, head_ones)
        sink_lanes = jnp.repeat(c_sinks[l] * LOG2E, HEAD_DIM).reshape(C_WIDTH // HEAD_CHUNK, 1, HEAD_CHUNK)
        casts = [(w_gate, l), (w_up, l), (w_down, l)] + ([(w_in, l + 1)] if l + 1 < depth else [])
        xf, wg, wu, wd, *nxt = _out_proj(oa, ob, rest, kvr, sink_lanes, xf, w_out_l, casts,
                                         batch=b, tm=_tile(s, 512))
        if nxt:
            w_in_l = nxt[0]
        xf = _ffn(xf, norm2_g[l].reshape(1, d), wg, wu, wd, tm=_tile(m, 1024), tf=512)
    return xf.reshape(b, s, d)
```
